```python
import math
import jax, jax.numpy as jnp
from jax import lax
import numpy as np

D_MODEL = 1024
BATCH = 8
SEQ = 2048
DEPTH = 4

CONV_CH = D_MODEL // 2
CONV_WIDTH = 31
DIFF_QK_DIM = 64
DIFF_V_DIM = 2 * DIFF_QK_DIM
DIFF_HEADS = D_MODEL // 256
MOBA_HEAD_DIM = 64
MOBA_HEADS = D_MODEL // 128
MOBA_BLOCK = 256
MOBA_TOPK = 3
MOBA_Q_CHUNK = 32
ATTN_Q_BLOCK = 128
D_FF = 4 * D_MODEL
ROPE_THETA = 10000.0
N_BRANCHES = 3
LN_EPS = 1e-5
NEG = -1e30
DEEPNORM_ALPHA = (2.0 * DEPTH) ** 0.25
DEEPNORM_BETA = (8.0 * DEPTH) ** -0.25

COLS_CONV = 2 * CONV_CH
COLS_DIFF_QK = 2 * DIFF_HEADS * DIFF_QK_DIM
COLS_DIFF_V = DIFF_HEADS * DIFF_V_DIM
COLS_MOBA = MOBA_HEADS * MOBA_HEAD_DIM
COLS_GATE = N_BRANCHES * D_MODEL
IN_COLS = COLS_CONV + 2 * COLS_DIFF_QK + COLS_DIFF_V + 3 * COLS_MOBA + COLS_GATE

kernel_name = "hybrid_conv_diffattn_moba_deepnorm"


def _split_offsets():
    sizes = [COLS_CONV, COLS_DIFF_QK, COLS_DIFF_QK, COLS_DIFF_V, COLS_MOBA, COLS_MOBA, COLS_MOBA]
    offs, acc = [], 0
    for s in sizes:
        acc += s
        offs.append(acc)
    return offs


def layer_norm(x, g, b):
    xf = x.astype(jnp.float32)
    mu = jnp.mean(xf, axis=-1, keepdims=True)
    var = jnp.mean(jnp.square(xf - mu), axis=-1, keepdims=True)
    y = (xf - mu) * lax.rsqrt(var + LN_EPS) * g.astype(jnp.float32) + b.astype(jnp.float32)
    return y.astype(x.dtype)


def rms_norm(x, g):
    xf = x.astype(jnp.float32)
    y = xf * lax.rsqrt(jnp.mean(jnp.square(xf), axis=-1, keepdims=True) + LN_EPS) * g.astype(jnp.float32)
    return y.astype(x.dtype)


def rope_tables(seq, dim):
    pos = jnp.arange(seq, dtype=jnp.float32)
    inv = ROPE_THETA ** (-jnp.arange(0, dim, 2, dtype=jnp.float32) / dim)
    ang = pos[:, None] * inv[None, :]
    ang = jnp.concatenate([ang, ang], axis=-1)
    return jnp.cos(ang), jnp.sin(ang)


def apply_rope(x, cos, sin):
    c = cos[None, :, None, :].astype(x.dtype)
    s = sin[None, :, None, :].astype(x.dtype)
    x1, x2 = jnp.split(x, 2, axis=-1)
    return x * c + jnp.concatenate([-x2, x1], axis=-1) * s


def conv_branch(u, conv_w, conv_b, ln_g, ln_b, w_out):
    a, gt = jnp.split(u, 2, axis=-1)
    h = a * jax.nn.sigmoid(gt)
    h = lax.conv_general_dilated(
        h, conv_w.astype(h.dtype), window_strides=(1,),
        padding=[(CONV_WIDTH - 1, 0)],
        dimension_numbers=("NWC", "WIO", "NWC"),
        feature_group_count=CONV_CH) + conv_b
    h = jax.nn.silu(layer_norm(h, ln_g, ln_b))
    return h @ w_out


def diff_attention(q, k, v, lq1, lk1, lq2, lk2, subln_g, layer_idx, cos, sin, w_out):
    B, S, _ = q.shape
    H, d = DIFF_HEADS, DIFF_QK_DIM
    q = apply_rope(q.reshape(B, S, 2 * H, d), cos, sin)
    k = apply_rope(k.reshape(B, S, 2 * H, d), cos, sin)
    q = q.transpose(0, 2, 1, 3).reshape(B, H, 2, S, d)
    k = k.transpose(0, 2, 1, 3).reshape(B, H, 2, S, d)
    v = v.reshape(B, S, H, DIFF_V_DIM).transpose(0, 2, 1, 3)
    lambda_init = 0.8 - 0.6 * math.exp(-0.3 * layer_idx)
    f32 = jnp.float32
    lam = (jnp.exp(jnp.sum(lq1.astype(f32) * lk1.astype(f32)))
           - jnp.exp(jnp.sum(lq2.astype(f32) * lk2.astype(f32))) + lambda_init)
    scale = d ** -0.5
    nqb = S // ATTN_Q_BLOCK
    qb = q.reshape(B, H, 2, nqb, ATTN_Q_BLOCK, d).transpose(3, 0, 1, 2, 4, 5)
    key_pos = jnp.arange(S)

    def one_block(args):
        qblk, bi = args
        s = jnp.einsum("bhtqd,bhtkd->bhtqk", qblk, k).astype(f32) * scale
        qpos = bi * ATTN_Q_BLOCK + jnp.arange(ATTN_Q_BLOCK)
        s = jnp.where(key_pos[None, :] <= qpos[:, None], s, NEG)
        p = jax.nn.softmax(s, axis=-1)
        a = p[:, :, 0] - lam * p[:, :, 1]
        return jnp.einsum("bhqk,bhkv->bhqv", a.astype(v.dtype), v)

    o = lax.map(one_block, (qb, jnp.arange(nqb)))
    o = o.transpose(1, 0, 3, 2, 4).reshape(B, S, H, DIFF_V_DIM)
    o = rms_norm(o, subln_g) * (1.0 - lambda_init)
    return o.reshape(B, S, H * DIFF_V_DIM) @ w_out


def moba_attention(q, k, v, cos, sin, w_out):
    B, S, _ = q.shape
    H, hd, BL, QC = MOBA_HEADS, MOBA_HEAD_DIM, MOBA_BLOCK, MOBA_Q_CHUNK
    q = apply_rope(q.reshape(B, S, H, hd), cos, sin).transpose(0, 2, 1, 3)
    k = apply_rope(k.reshape(B, S, H, hd), cos, sin).transpose(0, 2, 1, 3)
    v = v.reshape(B, S, H, hd).transpose(0, 2, 1, 3)
    nb = -(-S // BL)
    pad = nb * BL - S
    kb = jnp.pad(k, ((0, 0), (0, 0), (0, pad), (0, 0))).reshape(B, H, nb, BL, hd)
    vb = jnp.pad(v, ((0, 0), (0, 0), (0, pad), (0, 0))).reshape(B, H, nb, BL, hd)
    kmean = jnp.mean(kb, axis=3)
    topk = min(MOBA_TOPK, nb)
    scale = hd ** -0.5
    f32 = jnp.float32
    nch = S // QC
    qc = q.reshape(B, H, nch, QC, hd).transpose(2, 0, 1, 3, 4)
    bidx = jnp.arange(B)[:, None, None, None]
    hidx = jnp.arange(H)[None, :, None, None]
    blk_ids = jnp.arange(nb)

    def one_chunk(args):
        qch, ci = args
        start = ci * QC
        own = start // BL
        qpos = start + jnp.arange(QC)
        gate = jnp.einsum("bhqd,bhnd->bhqn", qch, kmean).astype(f32)
        gate = jnp.where(blk_ids < own, gate, NEG)
        _, idx = lax.top_k(gate, topk)
        valid = idx < own
        k_sel = kb[bidx, hidx, idx]
        v_sel = vb[bidx, hidx, idx]
        s_sel = jnp.einsum("bhqd,bhqnjd->bhqnj", qch, k_sel).astype(f32) * scale
        s_sel = jnp.where(valid[..., None], s_sel, NEG).reshape(B, H, QC, topk * BL)
        k_own = lax.dynamic_index_in_dim(kb, own, axis=2, keepdims=False)
        v_own = lax.dynamic_index_in_dim(vb, own, axis=2, keepdims=False)
        s_own = jnp.einsum("bhqd,bhjd->bhqj", qch, k_own).astype(f32) * scale
        kpos = own * BL + jnp.arange(BL)
        s_own = jnp.where(kpos[None, :] <= qpos[:, None], s_own, NEG)
        p = jax.nn.softmax(jnp.concatenate([s_sel, s_own], axis=-1), axis=-1).astype(v.dtype)
        p_sel = p[..., :topk * BL].reshape(B, H, QC, topk, BL)
        p_own = p[..., topk * BL:]
        return (jnp.einsum("bhqnj,bhqnjd->bhqd", p_sel, v_sel)
                + jnp.einsum("bhqj,bhjd->bhqd", p_own, v_own))

    o = lax.map(one_chunk, (qc, jnp.arange(nch)))
    o = o.transpose(1, 0, 3, 2, 4).reshape(B, S, H * hd)
    return o @ w_out


def setup_inputs(seed: int = 0) -> dict:
    key = jax.random.key(seed)
    ks = jax.random.split(key, 26)
    n = jax.random.normal
    L, D = DEPTH, D_MODEL
    beta = DEEPNORM_BETA
    return {
        "x": n(ks[0], (BATCH, SEQ, D), jnp.float32),
        "w_in": n(ks[1], (L, D, IN_COLS), jnp.float32) * D ** -0.5,
        "b_gate": 0.02 * n(ks[2], (L, COLS_GATE), jnp.float32),
        "conv_w": n(ks[3], (L, CONV_WIDTH, 1, CONV_CH), jnp.float32) * CONV_WIDTH ** -0.5,
        "conv_b": 0.02 * n(ks[4], (L, CONV_CH), jnp.float32),
        "conv_ln_g": 1.0 + 0.02 * n(ks[5], (L, CONV_CH), jnp.float32),
        "conv_ln_b": 0.02 * n(ks[6], (L, CONV_CH), jnp.float32),
        "w_conv_out": n(ks[7], (L, CONV_CH, D), jnp.float32) * CONV_CH ** -0.5 * beta,
        "lam_q1": 0.1 * n(ks[8], (L, DIFF_QK_DIM), jnp.float32),
        "lam_k1": 0.1 * n(ks[9], (L, DIFF_QK_DIM), jnp.float32),
        "lam_q2": 0.1 * n(ks[10], (L, DIFF_QK_DIM), jnp.float32),
        "lam_k2": 0.1 * n(ks[11], (L, DIFF_QK_DIM), jnp.float32),
        "diff_subln_g": 1.0 + 0.02 * n(ks[12], (L, DIFF_V_DIM), jnp.float32),
        "w_diff_out": n(ks[13], (L, COLS_DIFF_V, D), jnp.float32) * COLS_DIFF_V ** -0.5 * beta,
        "w_moba_out": n(ks[14], (L, COLS_MOBA, D), jnp.float32) * COLS_MOBA ** -0.5 * beta,
        "w_o": n(ks[15], (L, D, D), jnp.float32) * D ** -0.5 * beta,
        "ln1_g": 1.0 + 0.02 * n(ks[16], (L, D), jnp.float32),
        "ln1_b": 0.02 * n(ks[17], (L, D), jnp.float32),
        "w_ff1": n(ks[18], (L, D, D_FF), jnp.float32) * D ** -0.5 * beta,
        "b_ff1": 0.02 * n(ks[19], (L, D_FF), jnp.float32),
        "w_ff2": n(ks[20], (L, D_FF, D), jnp.float32) * D_FF ** -0.5 * beta,
        "b_ff2": 0.02 * n(ks[21], (L, D), jnp.float32),
        "ln2_g": 1.0 + 0.02 * n(ks[22], (L, D), jnp.float32),
        "ln2_b": 0.02 * n(ks[23], (L, D), jnp.float32),
    }


def reference(x, w_in, b_gate, conv_w, conv_b, conv_ln_g, conv_ln_b, w_conv_out,
              lam_q1, lam_k1, lam_q2, lam_k2, diff_subln_g, w_diff_out, w_moba_out, w_o,
              ln1_g, ln1_b, w_ff1, b_ff1, w_ff2, b_ff2, ln2_g, ln2_b):
    B, S, D = x.shape
    cos, sin = rope_tables(S, DIFF_QK_DIM)
    offs = _split_offsets()
    for l in range(DEPTH):
        u = x @ w_in[l]
        u_conv, dq, dk, dv, mq, mk, mv, g = jnp.split(u, offs, axis=-1)
        y_a = conv_branch(u_conv, conv_w[l], conv_b[l], conv_ln_g[l], conv_ln_b[l], w_conv_out[l])
        y_b = diff_attention(dq, dk, dv, lam_q1[l], lam_k1[l], lam_q2[l], lam_k2[l],
                             diff_subln_g[l], l, cos, sin, w_diff_out[l])
        y_c = moba_attention(mq, mk, mv, cos, sin, w_moba_out[l])
        gates = jax.nn.sigmoid(g + b_gate[l]).reshape(B, S, N_BRANCHES, D)
        m = gates[:, :, 0] * y_a + gates[:, :, 1] * y_b + gates[:, :, 2] * y_c
        x = layer_norm(DEEPNORM_ALPHA * x + m @ w_o[l], ln1_g[l], ln1_b[l])
        h = jnp.square(jax.nn.relu(x @ w_ff1[l] + b_ff1[l]))
        x = layer_norm(DEEPNORM_ALPHA * x + h @ w_ff2[l] + b_ff2[l], ln2_g[l], ln2_b[l])
    return x
```

```python
import functools
import math

import jax
import jax.numpy as jnp
from jax import lax
from jax.experimental import pallas as pl
from jax.experimental.pallas import tpu as pltpu

F32 = jnp.float32
BF16 = jnp.bfloat16

D_MODEL = 1024
DEPTH = 4
CONV_CH = D_MODEL // 2
CONV_WIDTH = 31
DIFF_QK_DIM = 64
DIFF_V_DIM = 2 * DIFF_QK_DIM
DIFF_HEADS = D_MODEL // 256
MOBA_HEAD_DIM = 64
MOBA_HEADS = D_MODEL // 128
MOBA_BLOCK = 256
MOBA_TOPK = 3
D_FF = 4 * D_MODEL
ROPE_THETA = 10000.0
N_BRANCHES = 3
LN_EPS = 1e-5
NEG = -1e30
DEEPNORM_ALPHA = (2.0 * DEPTH) ** 0.25

BRANCH_W = 512
PROJ_COLS = 8 * BRANCH_W
LANES = 128
HALF = DIFF_QK_DIM // 2
ATT_BLOCK = 256
CONV_TILE = 256
CONV_HALO = 32
VMEM_LIMIT = 56 * 1024 * 1024


def _params(n_axes):
    return pltpu.CompilerParams(dimension_semantics=("arbitrary",) * n_axes,
                                vmem_limit_bytes=VMEM_LIMIT)


def _sigmoid(x):
    return 1.0 / (1.0 + jnp.exp(-x))


def _layer_norm(y, g, b):
    mu = jnp.mean(y, axis=-1, keepdims=True)
    d = y - mu
    var = jnp.mean(d * d, axis=-1, keepdims=True)
    return d * lax.rsqrt(var + LN_EPS) * g + b


def _dot(a, b):
    return jnp.dot(a, b, preferred_element_type=F32)


def _dot_nt(a, b):
    return lax.dot_general(a, b, (((1,), (1,)), ((), ())), preferred_element_type=F32)


def _rope(u, c, sa, sb):
    outs = []
    for j in range(u.shape[1] // LANES):
        uc = u[:, LANES * j:LANES * (j + 1)]
        fwd = pltpu.roll(uc, HALF, 1)
        bwd = pltpu.roll(uc, LANES - HALF, 1)
        outs.append(uc * c + bwd * sa + fwd * sb)
    return jnp.concatenate(outs, axis=1)


def _proj_kernel(x_ref, w_ref, cq_ref, sqa_ref, sqb_ref, ck_ref, ska_ref, skb_ref,
                 h_ref, dq_ref, dk_ref, dv_ref, mq_ref, mk_ref, mv_ref):
    xb = x_ref[...].astype(BF16)

    def mm(group):
        return _dot(xb, w_ref[:, group * BRANCH_W:(group + 1) * BRANCH_W])

    h_ref[...] = mm(0) * _sigmoid(mm(1))
    cq, sqa, sqb = cq_ref[...], sqa_ref[...], sqb_ref[...]
    ck, ska, skb = ck_ref[...], ska_ref[...], skb_ref[...]
    dq_ref[...] = _rope(mm(2), cq, sqa, sqb).astype(BF16)
    dk_ref[...] = _rope(mm(3), ck, ska, skb).astype(BF16)
    dv_ref[...] = mm(4).astype(BF16)
    mq_ref[...] = _rope(mm(5), cq, sqa, sqb).astype(BF16)
    mk_ref[...] = _rope(mm(6), ck, ska, skb).astype(BF16)
    mv_ref[...] = mm(7).astype(BF16)


def _proj_call(x, w_p, tables, seq, tm=512):
    t = x.shape[0]
    n_seq_tiles = seq // tm
    row = lambda i: (i, 0)
    tab = lambda i: (i % n_seq_tiles, 0)
    out_spec = pl.BlockSpec((tm, BRANCH_W), row)
    return pl.pallas_call(
        _proj_kernel,
        grid=(t // tm,),
        in_specs=[pl.BlockSpec((tm, D_MODEL), row),
                  pl.BlockSpec((D_MODEL, PROJ_COLS), lambda i: (0, 0))]
                 + [pl.BlockSpec((tm, LANES), tab)] * 6,
        out_specs=[out_spec] * 7,
        out_shape=[jax.ShapeDtypeStruct((t, BRANCH_W), F32)]
                  + [jax.ShapeDtypeStruct((t, BRANCH_W), BF16)] * 6,
        compiler_params=_params(1),
        name="proj",
    )(x, w_p, *tables)


def _conv_kernel(prev_ref, cur_ref, w_ref, b_ref, g_ref, beta_ref, o_ref, pad_ref):
    i = pl.program_id(1)
    keep = (i > 0).astype(F32)
    pad_ref[0:CONV_HALO, :] = prev_ref[...] * keep
    pad_ref[CONV_HALO:CONV_HALO + CONV_TILE, :] = cur_ref[...]
    base = CONV_HALO - (CONV_WIDTH - 1)
    acc = jnp.zeros((CONV_TILE, CONV_CH), F32) + b_ref[...]
    for k in range(CONV_WIDTH):
        acc = acc + pad_ref[base + k:base + k + CONV_TILE, :] * w_ref[k:k + 1, :]
    y = _layer_norm(acc, g_ref[...], beta_ref[...])
    o_ref[...] = (y * _sigmoid(y)).astype(BF16)


def _conv_call(h, conv_w, conv_b, ln_g, ln_b, batch, seq):
    t = h.shape[0]
    n_tiles = seq // CONV_TILE
    halo_per_tile = CONV_TILE // CONV_HALO
    cur = lambda b, i: (b * n_tiles + i, 0)
    prev = lambda b, i: (jnp.maximum((b * n_tiles + i) * halo_per_tile - 1, 0), 0)
    vec = pl.BlockSpec((1, CONV_CH), lambda b, i: (0, 0))
    return pl.pallas_call(
        _conv_kernel,
        grid=(batch, n_tiles),
        in_specs=[pl.BlockSpec((CONV_HALO, CONV_CH), prev),
                  pl.BlockSpec((CONV_TILE, CONV_CH), cur),
                  pl.BlockSpec((CONV_WIDTH, CONV_CH), lambda b, i: (0, 0)),
                  vec, vec, vec],
        out_specs=pl.BlockSpec((CONV_TILE, CONV_CH), cur),
        out_shape=jax.ShapeDtypeStruct((t, CONV_CH), BF16),
        scratch_shapes=[pltpu.VMEM((CONV_HALO + CONV_TILE, CONV_CH), F32)],
        compiler_params=_params(2),
        name="conv",
    )(h, h, conv_w, conv_b, ln_g, ln_b)


def _flash_update(carry, s, v_blk):
    m, l, acc = carry
    m_new = jnp.maximum(m, jnp.max(s, axis=-1, keepdims=True))
    alpha = jnp.exp(m - m_new)
    p = jnp.exp(s - m_new)
    l_new = alpha * l + jnp.sum(p, axis=-1, keepdims=True)
    acc_new = alpha * acc + _dot(p.astype(BF16), v_blk)
    return m_new, l_new, acc_new


def _flash_init(width):
    return (jnp.full((ATT_BLOCK, 1), NEG, F32), jnp.zeros((ATT_BLOCK, 1), F32),
            jnp.zeros((ATT_BLOCK, width), F32))


def _causal_mask():
    r = lax.broadcasted_iota(jnp.int32, (ATT_BLOCK, ATT_BLOCK), 0)
    c = lax.broadcasted_iota(jnp.int32, (ATT_BLOCK, ATT_BLOCK), 1)
    return c <= r


def _split_halves(q):
    lane = lax.broadcasted_iota(jnp.int32, q.shape, 1)
    qf = q.astype(F32)
    lo = jnp.where(lane < DIFF_QK_DIM, qf, 0.0)
    hi = jnp.where(lane >= DIFF_QK_DIM, qf, 0.0)
    return lo, hi


def _diff_kernel(q_ref, k_ref, v_ref, lq1_ref, lk1_ref, lq2_ref, lk2_ref, g_ref, o_ref, *, lambda_init):
    i = pl.program_id(2)
    q1f, q2f = _split_halves(q_ref[...])
    q1, q2 = q1f.astype(BF16), q2f.astype(BF16)

    def block(kb, carry, mask):
        start = pl.multiple_of(kb * ATT_BLOCK, ATT_BLOCK)
        k_blk = k_ref[pl.ds(start, ATT_BLOCK), :]
        v_blk = v_ref[pl.ds(start, ATT_BLOCK), :]
        s1, s2 = _dot_nt(q1, k_blk), _dot_nt(q2, k_blk)
        if mask is not None:
            s1, s2 = jnp.where(mask, s1, NEG), jnp.where(mask, s2, NEG)
        c1, c2 = carry
        return _flash_update(c1, s1, v_blk), _flash_update(c2, s2, v_blk)

    carry = (_flash_init(DIFF_V_DIM), _flash_init(DIFF_V_DIM))
    carry = block(i, carry, _causal_mask())
    carry = lax.fori_loop(0, i, lambda kb, c: block(kb, c, None), carry)
    (_, l1, a1), (_, l2, a2) = carry

    lam = (jnp.exp(jnp.sum(lq1_ref[...] * lk1_ref[...], axis=-1, keepdims=True))
           - jnp.exp(jnp.sum(lq2_ref[...] * lk2_ref[...], axis=-1, keepdims=True)) + lambda_init)
    o = a1 / l1 - lam * (a2 / l2)
    o = o * lax.rsqrt(jnp.mean(o * o, axis=-1, keepdims=True) + LN_EPS) * g_ref[...]
    o_ref[...] = (o * (1.0 - lambda_init)).astype(BF16)


def _diff_call(dq, dk, dv, lq1, lk1, lq2, lk2, subln_g, layer_idx, batch, seq):
    t = dq.shape[0]
    nq = seq // ATT_BLOCK
    lambda_init = 0.8 - 0.6 * math.exp(-0.3 * layer_idx)
    qo = lambda b, h, i: (b * nq + i, h)
    kv = lambda b, h, i: (b, h)
    lam_spec = pl.BlockSpec((1, DIFF_QK_DIM), lambda b, h, i: (0, 0))
    return pl.pallas_call(
        functools.partial(_diff_kernel, lambda_init=lambda_init),
        grid=(batch, DIFF_HEADS, nq),
        in_specs=[pl.BlockSpec((ATT_BLOCK, LANES), qo),
                  pl.BlockSpec((seq, LANES), kv),
                  pl.BlockSpec((seq, LANES), kv),
                  lam_spec, lam_spec, lam_spec, lam_spec,
                  pl.BlockSpec((1, DIFF_V_DIM), lambda b, h, i: (0, 0))],
        out_specs=pl.BlockSpec((ATT_BLOCK, LANES), qo),
        out_shape=jax.ShapeDtypeStruct((t, BRANCH_W), BF16),
        compiler_params=_params(3),
        name="diff_attn",
    )(dq, dk, dv, lq1, lk1, lq2, lk2, subln_g)


def _moba_select(gate, own):
    lane = lax.broadcasted_iota(jnp.int32, gate.shape, 1)
    lane_f = lane.astype(F32)
    neg_inf = -jnp.inf
    gv = jnp.where(lane < own, gate, neg_inf)
    bias = jnp.full(gate.shape, NEG, F32)
    for _ in range(MOBA_TOPK):
        mx = jnp.max(gv, axis=-1, keepdims=True)
        is_mx = jnp.logical_and(gv == mx, gv > neg_inf)
        idx = jnp.min(jnp.where(is_mx, lane_f, float(LANES)), axis=-1, keepdims=True)
        pick = lane_f == idx
        bias = jnp.where(pick, 0.0, bias)
        gv = jnp.where(pick, neg_inf, gv)
    return bias


def _moba_kernel(q_ref, k_ref, v_ref, o_ref, kmean_ref):
    own = pl.program_id(2)
    n_blocks = k_ref.shape[0] // MOBA_BLOCK
    qe_f, qo_f = _split_halves(q_ref[...])
    qe, qo = qe_f.astype(BF16), qo_f.astype(BF16)

    kmean_ref[...] = jnp.zeros(kmean_ref.shape, F32)
    for n in range(n_blocks):
        k_rows = k_ref[n * MOBA_BLOCK:(n + 1) * MOBA_BLOCK, :].astype(F32)
        kmean_ref[n:n + 1, :] = jnp.sum(k_rows, axis=0, keepdims=True) * (1.0 / MOBA_BLOCK)
    kmean = kmean_ref[...]

    def gate(qh):
        return lax.dot_general(qh, kmean, (((1,), (1,)), ((), ())), preferred_element_type=F32,
                               precision=lax.Precision.HIGHEST)

    bias_e = _moba_select(gate(qe_f), own)
    bias_o = _moba_select(gate(qo_f), own)
    lane = lax.broadcasted_iota(jnp.int32, (ATT_BLOCK, LANES), 1)

    def block(kb, carry, mask):
        start = pl.multiple_of(kb * ATT_BLOCK, ATT_BLOCK)
        k_blk = k_ref[pl.ds(start, ATT_BLOCK), :]
        v_blk = v_ref[pl.ds(start, ATT_BLOCK), :]
        se, so = _dot_nt(qe, k_blk), _dot_nt(qo, k_blk)
        if mask is not None:
            se, so = jnp.where(mask, se, NEG), jnp.where(mask, so, NEG)
        else:
            at_kb = lane == kb
            se = se + jnp.sum(jnp.where(at_kb, bias_e, 0.0), axis=-1, keepdims=True)
            so = so + jnp.sum(jnp.where(at_kb, bias_o, 0.0), axis=-1, keepdims=True)
        ce, co = carry
        return _flash_update(ce, se, v_blk), _flash_update(co, so, v_blk)

    carry = (_flash_init(LANES), _flash_init(LANES))
    carry = block(own, carry, _causal_mask())
    carry = lax.fori_loop(0, own, lambda kb, c: block(kb, c, None), carry)
    (_, le, ae), (_, lo, ao) = carry
    o_ref[...] = jnp.where(lane < MOBA_HEAD_DIM, ae / le, ao / lo).astype(BF16)


def _moba_call(mq, mk, mv, batch, seq):
    t = mq.shape[0]
    nq = seq // ATT_BLOCK
    qo = lambda b, h, i: (b * nq + i, h)
    kv = lambda b, h, i: (b, h)
    return pl.pallas_call(
        _moba_kernel,
        grid=(batch, MOBA_HEADS // 2, nq),
        in_specs=[pl.BlockSpec((ATT_BLOCK, LANES), qo),
                  pl.BlockSpec((seq, LANES), kv),
                  pl.BlockSpec((seq, LANES), kv)],
        out_specs=pl.BlockSpec((ATT_BLOCK, LANES), qo),
        out_shape=jax.ShapeDtypeStruct((t, BRANCH_W), BF16),
        scratch_shapes=[pltpu.VMEM((LANES, LANES), F32)],
        compiler_params=_params(3),
        name="moba",
    )(mq, mk, mv)


def _merge_kernel(x_ref, fa_ref, fb_ref, fc_ref, wg_ref, bg_ref, wa_ref, wb_ref, wc_ref, wo_ref,
                  g_ref, b_ref, o_ref):
    x = x_ref[...]
    xb = x.astype(BF16)
    m = None
    for j, (f_ref, w_ref) in enumerate(((fa_ref, wa_ref), (fb_ref, wb_ref), (fc_ref, wc_ref))):
        cols = slice(j * D_MODEL, (j + 1) * D_MODEL)
        gate = _sigmoid(_dot(xb, wg_ref[:, cols]) + bg_ref[:, cols])
        term = gate * _dot(f_ref[...], w_ref[...])
        m = term if m is None else m + term
    z = _dot(m.astype(BF16), wo_ref[...])
    o_ref[...] = _layer_norm(DEEPNORM_ALPHA * x + z, g_ref[...], b_ref[...])


def _merge_call(x, fa, fb, fc, w_gate, b_gate, w_a, w_b, w_c, w_o, ln_g, ln_b, tm=256):
    t = x.shape[0]
    row = lambda i: (i, 0)
    full = lambda shape: pl.BlockSpec(shape, lambda i: (0, 0))
    feat = pl.BlockSpec((tm, BRANCH_W), row)
    return pl.pallas_call(
        _merge_kernel,
        grid=(t // tm,),
        in_specs=[pl.BlockSpec((tm, D_MODEL), row), feat, feat, feat,
                  full((D_MODEL, N_BRANCHES * D_MODEL)), full((1, N_BRANCHES * D_MODEL)),
                  full((BRANCH_W, D_MODEL)), full((BRANCH_W, D_MODEL)), full((BRANCH_W, D_MODEL)),
                  full((D_MODEL, D_MODEL)), full((1, D_MODEL)), full((1, D_MODEL))],
        out_specs=pl.BlockSpec((tm, D_MODEL), row),
        out_shape=jax.ShapeDtypeStruct((t, D_MODEL), F32),
        compiler_params=_params(1),
        name="merge",
    )(x, fa, fb, fc, w_gate, b_gate, w_a, w_b, w_c, w_o, ln_g, ln_b)


FF_CHUNK = 1024


def _ffn_kernel(x_ref, w1_ref, b1_ref, w2_ref, b2_ref, g_ref, b_ref, o_ref):
    x = x_ref[...]
    xb = x.astype(BF16)
    y = None
    for c in range(D_FF // FF_CHUNK):
        cols = slice(c * FF_CHUNK, (c + 1) * FF_CHUNK)
        h = jnp.maximum(_dot(xb, w1_ref[:, cols]) + b1_ref[:, cols], 0.0)
        part = _dot((h * h).astype(BF16), w2_ref[cols, :])
        y = part if y is None else y + part
    o_ref[...] = _layer_norm(DEEPNORM_ALPHA * x + y + b2_ref[...], g_ref[...], b_ref[...])


def _ffn_call(x, w1, b1, w2, b2, ln_g, ln_b, tm=256):
    t = x.shape[0]
    row = lambda i: (i, 0)
    full = lambda shape: pl.BlockSpec(shape, lambda i: (0, 0))
    return pl.pallas_call(
        _ffn_kernel,
        grid=(t // tm,),
        in_specs=[pl.BlockSpec((tm, D_MODEL), row),
                  full((D_MODEL, D_FF)), full((1, D_FF)), full((D_FF, D_MODEL)),
                  full((1, D_MODEL)), full((1, D_MODEL)), full((1, D_MODEL))],
        out_specs=pl.BlockSpec((tm, D_MODEL), row),
        out_shape=jax.ShapeDtypeStruct((t, D_MODEL), F32),
        compiler_params=_params(1),
        name="ffn",
    )(x, w1, b1, w2, b2, ln_g, ln_b)


def _rope_tables(seq):
    pos = jnp.arange(seq, dtype=F32)
    inv = ROPE_THETA ** (-jnp.arange(0, DIFF_QK_DIM, 2, dtype=F32) / DIFF_QK_DIM)
    ang = pos[:, None] * inv[None, :]
    ang = jnp.concatenate([ang, ang], axis=-1)
    cos, sin = jnp.cos(ang), jnp.sin(ang)
    first_half = (jnp.arange(DIFF_QK_DIM) < HALF)[None, :]
    sin_a = jnp.where(first_half, -sin, 0.0)
    sin_b = jnp.where(first_half, 0.0, sin)
    tile2 = lambda a: jnp.concatenate([a, a], axis=-1)
    k_tabs = [tile2(cos), tile2(sin_a), tile2(sin_b)]
    scale = DIFF_QK_DIM ** -0.5
    q_tabs = [a * scale for a in k_tabs]
    return q_tabs + k_tabs


def kernel(x, w_in, b_gate, conv_w, conv_b, conv_ln_g, conv_ln_b, w_conv_out, lam_q1, lam_k1, lam_q2, lam_k2,
           diff_subln_g, w_diff_out, w_moba_out, w_o, ln1_g, ln1_b, w_ff1, b_ff1, w_ff2, b_ff2, ln2_g, ln2_b):
    batch, seq, d = x.shape
    assert d == D_MODEL and seq % 512 == 0
    tables = _rope_tables(seq)
    row = lambda a: a.reshape(1, -1)
    xs = x.reshape(batch * seq, d)
    for l in range(DEPTH):
        w_p = w_in[l, :, :PROJ_COLS].astype(BF16)
        w_g = w_in[l, :, PROJ_COLS:].astype(BF16)
        h, dq, dk, dv, mq, mk, mv = _proj_call(xs, w_p, tables, seq)
        fa = _conv_call(h, conv_w[l, :, 0, :], row(conv_b[l]), row(conv_ln_g[l]), row(conv_ln_b[l]), batch, seq)
        fb = _diff_call(dq, dk, dv, row(lam_q1[l]), row(lam_k1[l]), row(lam_q2[l]), row(lam_k2[l]),
                        row(diff_subln_g[l]), l, batch, seq)
        fc = _moba_call(mq, mk, mv, batch, seq)
        xs = _merge_call(xs, fa, fb, fc, w_g, row(b_gate[l]),
                         w_conv_out[l].astype(BF16), w_diff_out[l].astype(BF16), w_moba_out[l].astype(BF16),
                         w_o[l].astype(BF16), row(ln1_g[l]), row(ln1_b[l]))
        xs = _ffn_call(xs, w_ff1[l].astype(BF16), row(b_ff1[l]), w_ff2[l].astype(BF16), row(b_ff2[l]),
                       row(ln2_g[l]), row(ln2_b[l]))
    return xs.reshape(batch, seq, d)
```

```python
import functools
import math

import jax
import jax.numpy as jnp
from jax import lax
from jax.experimental import pallas as pl
from jax.experimental.pallas import tpu as pltpu

F32 = jnp.float32
BF16 = jnp.bfloat16

D_MODEL = 1024
DEPTH = 4
CONV_CH = D_MODEL // 2
CONV_WIDTH = 31
DIFF_QK_DIM = 64
DIFF_V_DIM = 2 * DIFF_QK_DIM
DIFF_HEADS = D_MODEL // 256
MOBA_HEAD_DIM = 64
MOBA_HEADS = D_MODEL // 128
MOBA_BLOCK = 256
MOBA_TOPK = 3
D_FF = 4 * D_MODEL
ROPE_THETA = 10000.0
N_BRANCHES = 3
LN_EPS = 1e-5
NEG = -1e30
DEEPNORM_ALPHA = (2.0 * DEPTH) ** 0.25

BRANCH_W = 512
PROJ_COLS = 8 * BRANCH_W
LANES = 128
HALF = DIFF_QK_DIM // 2
ATT_BLOCK = 256
CONV_TILE = 256
CONV_HALO = 32
VMEM_LIMIT = 56 * 1024 * 1024


def _params(n_axes):
    return pltpu.CompilerParams(dimension_semantics=("arbitrary",) * n_axes,
                                vmem_limit_bytes=VMEM_LIMIT)


def _sigmoid(x):
    return 1.0 / (1.0 + jnp.exp(-x))


def _layer_norm(y, g, b):
    mu = jnp.mean(y, axis=-1, keepdims=True)
    d = y - mu
    var = jnp.mean(d * d, axis=-1, keepdims=True)
    return d * lax.rsqrt(var + LN_EPS) * g + b


def _dot(a, b):
    return jnp.dot(a, b, preferred_element_type=F32)


def _dot_nt(a, b):
    return lax.dot_general(a, b, (((1,), (1,)), ((), ())), preferred_element_type=F32)


def _rope(u, c, sa, sb):
    outs = []
    for j in range(u.shape[1] // LANES):
        uc = u[:, LANES * j:LANES * (j + 1)]
        fwd = pltpu.roll(uc, HALF, 1)
        bwd = pltpu.roll(uc, LANES - HALF, 1)
        outs.append(uc * c + bwd * sa + fwd * sb)
    return jnp.concatenate(outs, axis=1)


def _proj_kernel(x_ref, w_ref, cq_ref, sqa_ref, sqb_ref, ck_ref, ska_ref, skb_ref,
                 h_ref, dq_ref, dk_ref, dv_ref, mq_ref, mk_ref, mv_ref):
    xb = x_ref[...].astype(BF16)

    def mm(group):
        return _dot(xb, w_ref[:, group * BRANCH_W:(group + 1) * BRANCH_W])

    h_ref[...] = mm(0) * _sigmoid(mm(1))
    cq, sqa, sqb = cq_ref[...], sqa_ref[...], sqb_ref[...]
    ck, ska, skb = ck_ref[...], ska_ref[...], skb_ref[...]
    dq_ref[...] = _rope(mm(2), cq, sqa, sqb).astype(BF16)
    dk_ref[...] = _rope(mm(3), ck, ska, skb).astype(BF16)
    dv_ref[...] = mm(4).astype(BF16)
    mq_ref[...] = _rope(mm(5), cq, sqa, sqb).astype(BF16)
    mk_ref[...] = _rope(mm(6), ck, ska, skb).astype(BF16)
    mv_ref[...] = mm(7).astype(BF16)


def _proj_call(x, w_p, tables, seq, tm=512):
    t = x.shape[0]
    n_seq_tiles = seq // tm
    row = lambda i: (i, 0)
    tab = lambda i: (i % n_seq_tiles, 0)
    out_spec = pl.BlockSpec((tm, BRANCH_W), row)
    return pl.pallas_call(
        _proj_kernel,
        grid=(t // tm,),
        in_specs=[pl.BlockSpec((tm, D_MODEL), row),
                  pl.BlockSpec((D_MODEL, PROJ_COLS), lambda i: (0, 0))]
                 + [pl.BlockSpec((tm, LANES), tab)] * 6,
        out_specs=[out_spec] * 7,
        out_shape=[jax.ShapeDtypeStruct((t, BRANCH_W), F32)]
                  + [jax.ShapeDtypeStruct((t, BRANCH_W), BF16)] * 6,
        compiler_params=_params(1),
        name="proj",
    )(x, w_p, *tables)


def _conv_kernel(prev_ref, cur_ref, w_ref, b_ref, g_ref, beta_ref, o_ref, pad_ref):
    i = pl.program_id(1)
    keep = (i > 0).astype(F32)
    pad_ref[0:CONV_HALO, :] = prev_ref[...] * keep
    pad_ref[CONV_HALO:CONV_HALO + CONV_TILE, :] = cur_ref[...]
    base = CONV_HALO - (CONV_WIDTH - 1)
    acc = jnp.zeros((CONV_TILE, CONV_CH), F32) + b_ref[...]
    for k in range(CONV_WIDTH):
        acc = acc + pad_ref[base + k:base + k + CONV_TILE, :] * w_ref[k:k + 1, :]
    y = _layer_norm(acc, g_ref[...], beta_ref[...])
    o_ref[...] = (y * _sigmoid(y)).astype(BF16)


def _conv_call(h, conv_w, conv_b, ln_g, ln_b, batch, seq):
    t = h.shape[0]
    n_tiles = seq // CONV_TILE
    halo_per_tile = CONV_TILE // CONV_HALO
    cur = lambda b, i: (b * n_tiles + i, 0)
    prev = lambda b, i: (jnp.maximum((b * n_tiles + i) * halo_per_tile - 1, 0), 0)
    vec = pl.BlockSpec((1, CONV_CH), lambda b, i: (0, 0))
    return pl.pallas_call(
        _conv_kernel,
        grid=(batch, n_tiles),
        in_specs=[pl.BlockSpec((CONV_HALO, CONV_CH), prev),
                  pl.BlockSpec((CONV_TILE, CONV_CH), cur),
                  pl.BlockSpec((CONV_WIDTH, CONV_CH), lambda b, i: (0, 0)),
                  vec, vec, vec],
        out_specs=pl.BlockSpec((CONV_TILE, CONV_CH), cur),
        out_shape=jax.ShapeDtypeStruct((t, CONV_CH), BF16),
        scratch_shapes=[pltpu.VMEM((CONV_HALO + CONV_TILE, CONV_CH), F32)],
        compiler_params=_params(2),
        name="conv",
    )(h, h, conv_w, conv_b, ln_g, ln_b)


def _causal_mask():
    r = lax.broadcasted_iota(jnp.int32, (ATT_BLOCK, ATT_BLOCK), 0)
    c = lax.broadcasted_iota(jnp.int32, (ATT_BLOCK, ATT_BLOCK), 1)
    return c <= r


def _split_halves(q):
    lane = lax.broadcasted_iota(jnp.int32, q.shape, 1)
    qf = q.astype(F32)
    lo = jnp.where(lane < DIFF_QK_DIM, qf, 0.0)
    hi = jnp.where(lane >= DIFF_QK_DIM, qf, 0.0)
    return lo, hi


def _blk(kb):
    return slice(kb * ATT_BLOCK, (kb + 1) * ATT_BLOCK)


def _lane_fold(x, op):
    return op(x[:, :LANES], x[:, LANES:])


def _softmax_pv(s_ref, v_ref, m_part, n_kb):
    m = jnp.max(m_part, axis=-1, keepdims=True)
    l_part, acc = None, None
    for kb in range(n_kb):
        p = jnp.exp(s_ref[:, _blk(kb)] - m)
        folded = _lane_fold(p, jnp.add)
        l_part = folded if l_part is None else l_part + folded
        pv = _dot(p.astype(BF16), v_ref[_blk(kb), :])
        acc = pv if acc is None else acc + pv
    return acc, jnp.sum(l_part, axis=-1, keepdims=True)


def _diff_tile(tile, q_ref, k_ref, v_ref, lam, g_ref, o_ref, s1_ref, s2_ref, lambda_init):
    q1f, q2f = _split_halves(q_ref[...])
    q1, q2 = q1f.astype(BF16), q2f.astype(BF16)
    n_kb = tile + 1
    mp1, mp2 = None, None
    for kb in range(n_kb):
        k_blk = k_ref[_blk(kb), :]
        s1, s2 = _dot_nt(q1, k_blk), _dot_nt(q2, k_blk)
        if kb == tile:
            mask = _causal_mask()
            s1, s2 = jnp.where(mask, s1, NEG), jnp.where(mask, s2, NEG)
        s1_ref[:, _blk(kb)] = s1
        s2_ref[:, _blk(kb)] = s2
        f1, f2 = _lane_fold(s1, jnp.maximum), _lane_fold(s2, jnp.maximum)
        mp1 = f1 if mp1 is None else jnp.maximum(mp1, f1)
        mp2 = f2 if mp2 is None else jnp.maximum(mp2, f2)
    a1, l1 = _softmax_pv(s1_ref, v_ref, mp1, n_kb)
    a2, l2 = _softmax_pv(s2_ref, v_ref, mp2, n_kb)
    o = a1 / l1 - lam * (a2 / l2)
    o = o * lax.rsqrt(jnp.mean(o * o, axis=-1, keepdims=True) + LN_EPS) * g_ref[...]
    o_ref[...] = (o * (1.0 - lambda_init)).astype(BF16)


def _diff_kernel(q_ref, k_ref, v_ref, lq1_ref, lk1_ref, lq2_ref, lk2_ref, g_ref, o_ref, s1_ref, s2_ref,
                 *, lambda_init):
    lam = (jnp.exp(jnp.sum(lq1_ref[...] * lk1_ref[...], axis=-1, keepdims=True))
           - jnp.exp(jnp.sum(lq2_ref[...] * lk2_ref[...], axis=-1, keepdims=True)) + lambda_init)
    for tile in range(k_ref.shape[0] // ATT_BLOCK):
        @pl.when(pl.program_id(2) == tile)
        def _(tile=tile):
            _diff_tile(tile, q_ref, k_ref, v_ref, lam, g_ref, o_ref, s1_ref, s2_ref, lambda_init)


def _diff_call(dq, dk, dv, lq1, lk1, lq2, lk2, subln_g, layer_idx, batch, seq):
    t = dq.shape[0]
    nq = seq // ATT_BLOCK
    lambda_init = 0.8 - 0.6 * math.exp(-0.3 * layer_idx)
    qo = lambda b, h, i: (b * nq + i, h)
    kv = lambda b, h, i: (b, h)
    lam_spec = pl.BlockSpec((1, DIFF_QK_DIM), lambda b, h, i: (0, 0))
    return pl.pallas_call(
        functools.partial(_diff_kernel, lambda_init=lambda_init),
        grid=(batch, DIFF_HEADS, nq),
        in_specs=[pl.BlockSpec((ATT_BLOCK, LANES), qo),
                  pl.BlockSpec((seq, LANES), kv),
                  pl.BlockSpec((seq, LANES), kv),
                  lam_spec, lam_spec, lam_spec, lam_spec,
                  pl.BlockSpec((1, DIFF_V_DIM), lambda b, h, i: (0, 0))],
        out_specs=pl.BlockSpec((ATT_BLOCK, LANES), qo),
        out_shape=jax.ShapeDtypeStruct((t, BRANCH_W), BF16),
        scratch_shapes=[pltpu.VMEM((ATT_BLOCK, seq), F32), pltpu.VMEM((ATT_BLOCK, seq), F32)],
        compiler_params=_params(3),
        name="diff_attn",
    )(dq, dk, dv, lq1, lk1, lq2, lk2, subln_g)


def _moba_select(gate, own):
    lane = lax.broadcasted_iota(jnp.int32, gate.shape, 1)
    lane_f = lane.astype(F32)
    neg_inf = -jnp.inf
    gv = jnp.where(lane < own, gate, neg_inf)
    bias = jnp.full(gate.shape, NEG, F32)
    for _ in range(MOBA_TOPK):
        mx = jnp.max(gv, axis=-1, keepdims=True)
        is_mx = jnp.logical_and(gv == mx, gv > neg_inf)
        idx = jnp.min(jnp.where(is_mx, lane_f, float(LANES)), axis=-1, keepdims=True)
        pick = lane_f == idx
        bias = jnp.where(pick, 0.0, bias)
        gv = jnp.where(pick, neg_inf, gv)
    return bias


def _moba_tile(own, q_ref, k_ref, v_ref, o_ref, kmean_ref, se_ref, so_ref):
    qe_f, qo_f = _split_halves(q_ref[...])
    qe, qo = qe_f.astype(BF16), qo_f.astype(BF16)
    lane = lax.broadcasted_iota(jnp.int32, (ATT_BLOCK, LANES), 1)

    select = own > MOBA_TOPK
    if select:
        kmean_ref[...] = jnp.zeros(kmean_ref.shape, F32)
        for n in range(own):
            k_rows = k_ref[_blk(n), :].astype(F32)
            kmean_ref[n:n + 1, :] = jnp.sum(k_rows, axis=0, keepdims=True) * (1.0 / MOBA_BLOCK)
        kmean = kmean_ref[...]

        def gate(qh):
            return lax.dot_general(qh, kmean, (((1,), (1,)), ((), ())), preferred_element_type=F32,
                                   precision=lax.Precision.HIGHEST)

        bias_e = _moba_select(gate(qe_f), own)
        bias_o = _moba_select(gate(qo_f), own)

    mpe, mpo = None, None
    for kb in range(own + 1):
        k_blk = k_ref[_blk(kb), :]
        se, so = _dot_nt(qe, k_blk), _dot_nt(qo, k_blk)
        if kb == own:
            mask = _causal_mask()
            se, so = jnp.where(mask, se, NEG), jnp.where(mask, so, NEG)
        elif select:
            at_kb = lane == kb
            se = se + jnp.sum(jnp.where(at_kb, bias_e, 0.0), axis=-1, keepdims=True)
            so = so + jnp.sum(jnp.where(at_kb, bias_o, 0.0), axis=-1, keepdims=True)
        se_ref[:, _blk(kb)] = se
        so_ref[:, _blk(kb)] = so
        fe, fo = _lane_fold(se, jnp.maximum), _lane_fold(so, jnp.maximum)
        mpe = fe if mpe is None else jnp.maximum(mpe, fe)
        mpo = fo if mpo is None else jnp.maximum(mpo, fo)
    ae, le = _softmax_pv(se_ref, v_ref, mpe, own + 1)
    ao, lo = _softmax_pv(so_ref, v_ref, mpo, own + 1)
    o_ref[...] = jnp.where(lane < MOBA_HEAD_DIM, ae / le, ao / lo).astype(BF16)


def _moba_kernel(q_ref, k_ref, v_ref, o_ref, kmean_ref, se_ref, so_ref):
    for own in range(k_ref.shape[0] // MOBA_BLOCK):
        @pl.when(pl.program_id(2) == own)
        def _(own=own):
            _moba_tile(own, q_ref, k_ref, v_ref, o_ref, kmean_ref, se_ref, so_ref)


def _moba_call(mq, mk, mv, batch, seq):
    t = mq.shape[0]
    nq = seq // ATT_BLOCK
    qo = lambda b, h, i: (b * nq + i, h)
    kv = lambda b, h, i: (b, h)
    return pl.pallas_call(
        _moba_kernel,
        grid=(batch, MOBA_HEADS // 2, nq),
        in_specs=[pl.BlockSpec((ATT_BLOCK, LANES), qo),
                  pl.BlockSpec((seq, LANES), kv),
                  pl.BlockSpec((seq, LANES), kv)],
        out_specs=pl.BlockSpec((ATT_BLOCK, LANES), qo),
        out_shape=jax.ShapeDtypeStruct((t, BRANCH_W), BF16),
        scratch_shapes=[pltpu.VMEM((LANES, LANES), F32),
                        pltpu.VMEM((ATT_BLOCK, seq), F32), pltpu.VMEM((ATT_BLOCK, seq), F32)],
        compiler_params=_params(3),
        name="moba",
    )(mq, mk, mv)


def _merge_kernel(x_ref, fa_ref, fb_ref, fc_ref, wg_ref, bg_ref, wa_ref, wb_ref, wc_ref, wo_ref,
                  g_ref, b_ref, o_ref):
    x = x_ref[...]
    xb = x.astype(BF16)
    m = None
    for j, (f_ref, w_ref) in enumerate(((fa_ref, wa_ref), (fb_ref, wb_ref), (fc_ref, wc_ref))):
        cols = slice(j * D_MODEL, (j + 1) * D_MODEL)
        gate = _sigmoid(_dot(xb, wg_ref[:, cols]) + bg_ref[:, cols])
        term = gate * _dot(f_ref[...], w_ref[...])
        m = term if m is None else m + term
    z = _dot(m.astype(BF16), wo_ref[...])
    o_ref[...] = _layer_norm(DEEPNORM_ALPHA * x + z, g_ref[...], b_ref[...])


def _merge_call(x, fa, fb, fc, w_gate, b_gate, w_a, w_b, w_c, w_o, ln_g, ln_b, tm=256):
    t = x.shape[0]
    row = lambda i: (i, 0)
    full = lambda shape: pl.BlockSpec(shape, lambda i: (0, 0))
    feat = pl.BlockSpec((tm, BRANCH_W), row)
    return pl.pallas_call(
        _merge_kernel,
        grid=(t // tm,),
        in_specs=[pl.BlockSpec((tm, D_MODEL), row), feat, feat, feat,
                  full((D_MODEL, N_BRANCHES * D_MODEL)), full((1, N_BRANCHES * D_MODEL)),
                  full((BRANCH_W, D_MODEL)), full((BRANCH_W, D_MODEL)), full((BRANCH_W, D_MODEL)),
                  full((D_MODEL, D_MODEL)), full((1, D_MODEL)), full((1, D_MODEL))],
        out_specs=pl.BlockSpec((tm, D_MODEL), row),
        out_shape=jax.ShapeDtypeStruct((t, D_MODEL), F32),
        compiler_params=_params(1),
        name="merge",
    )(x, fa, fb, fc, w_gate, b_gate, w_a, w_b, w_c, w_o, ln_g, ln_b)


FF_CHUNK = 1024


def _ffn_kernel(x_ref, w1_ref, b1_ref, w2_ref, b2_ref, g_ref, b_ref, o_ref):
    x = x_ref[...]
    xb = x.astype(BF16)
    y = None
    for c in range(D_FF // FF_CHUNK):
        cols = slice(c * FF_CHUNK, (c + 1) * FF_CHUNK)
        h = jnp.maximum(_dot(xb, w1_ref[:, cols]) + b1_ref[:, cols], 0.0)
        part = _dot((h * h).astype(BF16), w2_ref[cols, :])
        y = part if y is None else y + part
    o_ref[...] = _layer_norm(DEEPNORM_ALPHA * x + y + b2_ref[...], g_ref[...], b_ref[...])


def _ffn_call(x, w1, b1, w2, b2, ln_g, ln_b, tm=256):
    t = x.shape[0]
    row = lambda i: (i, 0)
    full = lambda shape: pl.BlockSpec(shape, lambda i: (0, 0))
    return pl.pallas_call(
        _ffn_kernel,
        grid=(t // tm,),
        in_specs=[pl.BlockSpec((tm, D_MODEL), row),
                  full((D_MODEL, D_FF)), full((1, D_FF)), full((D_FF, D_MODEL)),
                  full((1, D_MODEL)), full((1, D_MODEL)), full((1, D_MODEL))],
        out_specs=pl.BlockSpec((tm, D_MODEL), row),
        out_shape=jax.ShapeDtypeStruct((t, D_MODEL), F32),
        compiler_params=_params(1),
        name="ffn",
    )(x, w1, b1, w2, b2, ln_g, ln_b)


def _rope_tables(seq):
    pos = jnp.arange(seq, dtype=F32)
    inv = ROPE_THETA ** (-jnp.arange(0, DIFF_QK_DIM, 2, dtype=F32) / DIFF_QK_DIM)
    ang = pos[:, None] * inv[None, :]
    ang = jnp.concatenate([ang, ang], axis=-1)
    cos, sin = jnp.cos(ang), jnp.sin(ang)
    first_half = (jnp.arange(DIFF_QK_DIM) < HALF)[None, :]
    sin_a = jnp.where(first_half, -sin, 0.0)
    sin_b = jnp.where(first_half, 0.0, sin)
    tile2 = lambda a: jnp.concatenate([a, a], axis=-1)
    k_tabs = [tile2(cos), tile2(sin_a), tile2(sin_b)]
    scale = DIFF_QK_DIM ** -0.5
    q_tabs = [a * scale for a in k_tabs]
    return q_tabs + k_tabs


def kernel(x, w_in, b_gate, conv_w, conv_b, conv_ln_g, conv_ln_b, w_conv_out, lam_q1, lam_k1, lam_q2, lam_k2,
           diff_subln_g, w_diff_out, w_moba_out, w_o, ln1_g, ln1_b, w_ff1, b_ff1, w_ff2, b_ff2, ln2_g, ln2_b):
    batch, seq, d = x.shape
    assert d == D_MODEL and seq % 512 == 0
    tables = _rope_tables(seq)
    row = lambda a: a.reshape(1, -1)
    xs = x.reshape(batch * seq, d)
    for l in range(DEPTH):
        w_p = w_in[l, :, :PROJ_COLS].astype(BF16)
        w_g = w_in[l, :, PROJ_COLS:].astype(BF16)
        h, dq, dk, dv, mq, mk, mv = _proj_call(xs, w_p, tables, seq)
        fa = _conv_call(h, conv_w[l, :, 0, :], row(conv_b[l]), row(conv_ln_g[l]), row(conv_ln_b[l]), batch, seq)
        fb = _diff_call(dq, dk, dv, row(lam_q1[l]), row(lam_k1[l]), row(lam_q2[l]), row(lam_k2[l]),
                        row(diff_subln_g[l]), l, batch, seq)
        fc = _moba_call(mq, mk, mv, batch, seq)
        xs = _merge_call(xs, fa, fb, fc, w_g, row(b_gate[l]),
                         w_conv_out[l].astype(BF16), w_diff_out[l].astype(BF16), w_moba_out[l].astype(BF16),
                         w_o[l].astype(BF16), row(ln1_g[l]), row(ln1_b[l]))
        xs = _ffn_call(xs, w_ff1[l].astype(BF16), row(b_ff1[l]), w_ff2[l].astype(BF16), row(b_ff2[l]),
                       row(ln2_g[l]), row(ln2_b[l]))
    return xs.reshape(batch, seq, d)
```

```python
import functools
import math

import jax
import jax.numpy as jnp
from jax import lax
from jax.experimental import pallas as pl
from jax.experimental.pallas import tpu as pltpu

F32 = jnp.float32
BF16 = jnp.bfloat16

D_MODEL = 1024
DEPTH = 4
CONV_CH = D_MODEL // 2
CONV_WIDTH = 31
DIFF_QK_DIM = 64
DIFF_V_DIM = 2 * DIFF_QK_DIM
DIFF_HEADS = D_MODEL // 256
MOBA_HEAD_DIM = 64
MOBA_HEADS = D_MODEL // 128
MOBA_BLOCK = 256
MOBA_TOPK = 3
D_FF = 4 * D_MODEL
ROPE_THETA = 10000.0
N_BRANCHES = 3
LN_EPS = 1e-5
NEG = -1e30
DEEPNORM_ALPHA = (2.0 * DEPTH) ** 0.25

BRANCH_W = 512
PROJ_COLS = 8 * BRANCH_W
LANES = 128
SUBLANES = 8
HALF = DIFF_QK_DIM // 2
ATT_BLOCK = 256
SCORE_BUFFERS = 2
CONV_TILE = 256
CONV_HALO = 32
VMEM_LIMIT = 56 * 1024 * 1024


def _params(n_axes):
    return pltpu.CompilerParams(dimension_semantics=("arbitrary",) * n_axes,
                                vmem_limit_bytes=VMEM_LIMIT)


def _sigmoid(x):
    return 1.0 / (1.0 + jnp.exp(-x))


def _layer_norm(y, g, b):
    mu = jnp.mean(y, axis=-1, keepdims=True)
    d = y - mu
    var = jnp.mean(d * d, axis=-1, keepdims=True)
    return d * lax.rsqrt(var + LN_EPS) * g + b


def _dot(a, b):
    return jnp.dot(a, b, preferred_element_type=F32)


def _dot_nt(a, b):
    return lax.dot_general(a, b, (((1,), (1,)), ((), ())), preferred_element_type=F32)


def _rope(u, c, sa, sb):
    outs = []
    for j in range(u.shape[1] // LANES):
        uc = u[:, LANES * j:LANES * (j + 1)]
        fwd = pltpu.roll(uc, HALF, 1)
        bwd = pltpu.roll(uc, LANES - HALF, 1)
        outs.append(uc * c + bwd * sa + fwd * sb)
    return jnp.concatenate(outs, axis=1)


def _proj_kernel(x_ref, w_ref, cq_ref, sqa_ref, sqb_ref, ck_ref, ska_ref, skb_ref,
                 h_ref, dq_ref, dk_ref, dv_ref, mq_ref, mk_ref, mv_ref):
    xb = x_ref[...].astype(BF16)

    def mm(group):
        return _dot(xb, w_ref[:, group * BRANCH_W:(group + 1) * BRANCH_W])

    h_ref[...] = mm(0) * _sigmoid(mm(1))
    cq, sqa, sqb = cq_ref[...], sqa_ref[...], sqb_ref[...]
    ck, ska, skb = ck_ref[...], ska_ref[...], skb_ref[...]
    dq_ref[...] = _rope(mm(2), cq, sqa, sqb).astype(BF16)
    dk_ref[...] = _rope(mm(3), ck, ska, skb).astype(BF16)
    dv_ref[...] = mm(4).astype(BF16)
    mq_ref[...] = _rope(mm(5), cq, sqa, sqb).astype(BF16)
    mk_ref[...] = _rope(mm(6), ck, ska, skb).astype(BF16)
    mv_ref[...] = mm(7).astype(BF16)


def _proj_call(x, w_p, tables, seq, tm=512):
    t = x.shape[0]
    n_seq_tiles = seq // tm
    row = lambda i: (i, 0)
    tab = lambda i: (i % n_seq_tiles, 0)
    out_spec = pl.BlockSpec((tm, BRANCH_W), row)
    return pl.pallas_call(
        _proj_kernel,
        grid=(t // tm,),
        in_specs=[pl.BlockSpec((tm, D_MODEL), row),
                  pl.BlockSpec((D_MODEL, PROJ_COLS), lambda i: (0, 0))]
                 + [pl.BlockSpec((tm, LANES), tab)] * 6,
        out_specs=[out_spec] * 7,
        out_shape=[jax.ShapeDtypeStruct((t, BRANCH_W), F32)]
                  + [jax.ShapeDtypeStruct((t, BRANCH_W), BF16)] * 6,
        compiler_params=_params(1),
        name="proj",
    )(x, w_p, *tables)


def _conv_kernel(prev_ref, cur_ref, w_ref, b_ref, g_ref, beta_ref, o_ref, pad_ref):
    i = pl.program_id(1)
    keep = (i > 0).astype(F32)
    pad_ref[0:CONV_HALO, :] = prev_ref[...] * keep
    pad_ref[CONV_HALO:CONV_HALO + CONV_TILE, :] = cur_ref[...]
    base = CONV_HALO - (CONV_WIDTH - 1)
    acc = jnp.zeros((CONV_TILE, CONV_CH), F32) + b_ref[...]
    for r in range(SUBLANES):
        taps = [k for k in range(CONV_WIDTH) if (base + k) % SUBLANES == r]
        last = base + taps[-1]
        slab = pad_ref[r:last + CONV_TILE, :]
        part = None
        for k in taps:
            off = base + k - r
            term = slab[off:off + CONV_TILE, :] * w_ref[k:k + 1, :]
            part = term if part is None else part + term
        acc = acc + part
    y = _layer_norm(acc, g_ref[...], beta_ref[...])
    o_ref[...] = (y * _sigmoid(y)).astype(BF16)


def _conv_call(h, conv_w, conv_b, ln_g, ln_b, batch, seq):
    t = h.shape[0]
    n_tiles = seq // CONV_TILE
    halo_per_tile = CONV_TILE // CONV_HALO
    cur = lambda b, i: (b * n_tiles + i, 0)
    prev = lambda b, i: (jnp.maximum((b * n_tiles + i) * halo_per_tile - 1, 0), 0)
    vec = pl.BlockSpec((1, CONV_CH), lambda b, i: (0, 0))
    return pl.pallas_call(
        _conv_kernel,
        grid=(batch, n_tiles),
        in_specs=[pl.BlockSpec((CONV_HALO, CONV_CH), prev),
                  pl.BlockSpec((CONV_TILE, CONV_CH), cur),
                  pl.BlockSpec((CONV_WIDTH, CONV_CH), lambda b, i: (0, 0)),
                  vec, vec, vec],
        out_specs=pl.BlockSpec((CONV_TILE, CONV_CH), cur),
        out_shape=jax.ShapeDtypeStruct((t, CONV_CH), BF16),
        scratch_shapes=[pltpu.VMEM((CONV_HALO + CONV_TILE, CONV_CH), F32)],
        compiler_params=_params(2),
        name="conv",
    )(h, h, conv_w, conv_b, ln_g, ln_b)


MAPS = 2
ROWS = MAPS * ATT_BLOCK


def _blk(kb):
    return slice(kb * ATT_BLOCK, (kb + 1) * ATT_BLOCK)


def _causal_mask():
    r = lax.broadcasted_iota(jnp.int32, (ROWS, ATT_BLOCK), 0) % ATT_BLOCK
    c = lax.broadcasted_iota(jnp.int32, (ROWS, ATT_BLOCK), 1)
    return c <= r


def _fill_v_aug(v_ref, v_aug_ref):
    v_aug_ref[:, :LANES] = v_ref[...]
    v_aug_ref[:, LANES:] = jnp.ones((v_ref.shape[0], LANES), BF16)


def _attend(tile, score_fn, s_ref, p_ref, v_aug_ref):
    n_kb = tile + 1
    m_part = None
    for kb in range(n_kb):
        s = score_fn(kb)
        if kb == tile:
            s = jnp.where(_causal_mask(), s, NEG)
        s_ref[:, _blk(kb)] = s
        folded = jnp.maximum(s[:, :LANES], s[:, LANES:])
        m_part = folded if m_part is None else jnp.maximum(m_part, folded)
    m = jnp.max(m_part, axis=-1, keepdims=True)
    for kb in range(n_kb):
        p_ref[:, _blk(kb)] = jnp.exp2(s_ref[:, _blk(kb)] - m).astype(BF16)
    n_keys = n_kb * ATT_BLOCK
    acc = _dot(p_ref[:, :n_keys], v_aug_ref[:n_keys, :])
    return acc[:, :LANES] / acc[:, LANES:]


def _att_scratch(seq):
    return [pltpu.VMEM((SCORE_BUFFERS, ROWS, seq), F32),
            pltpu.VMEM((SCORE_BUFFERS, ROWS, seq), BF16),
            pltpu.VMEM((seq, 2 * LANES), BF16)]


def _diff_kernel(q_ref, k_ref, v_ref, lq1_ref, lk1_ref, lq2_ref, lk2_ref, g_ref, o_ref,
                 s_ref, p_ref, v_aug_ref, *, lambda_init):
    lam = (jnp.exp(jnp.sum(lq1_ref[...] * lk1_ref[...], axis=-1, keepdims=True))
           - jnp.exp(jnp.sum(lq2_ref[...] * lk2_ref[...], axis=-1, keepdims=True)) + lambda_init)
    _fill_v_aug(v_ref, v_aug_ref)
    lane = lax.broadcasted_iota(jnp.int32, (ATT_BLOCK, LANES), 1)
    for tile in reversed(range(k_ref.shape[0] // ATT_BLOCK)):
        qf = q_ref[_blk(tile), :].astype(F32)
        qq = jnp.concatenate([jnp.where(lane < DIFF_QK_DIM, qf, 0.0),
                              jnp.where(lane >= DIFF_QK_DIM, qf, 0.0)], axis=0).astype(BF16)
        buf = tile % SCORE_BUFFERS
        out = _attend(tile, lambda kb: _dot_nt(qq, k_ref[_blk(kb), :]), s_ref.at[buf], p_ref.at[buf], v_aug_ref)
        o = out[:ATT_BLOCK] - lam * out[ATT_BLOCK:]
        o = o * lax.rsqrt(jnp.mean(o * o, axis=-1, keepdims=True) + LN_EPS) * g_ref[...]
        o_ref[_blk(tile), :] = (o * (1.0 - lambda_init)).astype(BF16)


def _diff_call(dq, dk, dv, lq1, lk1, lq2, lk2, subln_g, layer_idx, batch, seq):
    t = dq.shape[0]
    lambda_init = 0.8 - 0.6 * math.exp(-0.3 * layer_idx)
    lam_spec = pl.BlockSpec((1, DIFF_QK_DIM), lambda b, h: (0, 0))
    seq_block = pl.BlockSpec((seq, LANES), lambda b, h: (b, h))
    return pl.pallas_call(
        functools.partial(_diff_kernel, lambda_init=lambda_init),
        grid=(batch, DIFF_HEADS),
        in_specs=[seq_block, seq_block, seq_block,
                  lam_spec, lam_spec, lam_spec, lam_spec,
                  pl.BlockSpec((1, DIFF_V_DIM), lambda b, h: (0, 0))],
        out_specs=seq_block,
        out_shape=jax.ShapeDtypeStruct((t, BRANCH_W), BF16),
        scratch_shapes=_att_scratch(seq),
        compiler_params=_params(2),
        name="diff_attn",
    )(dq, dk, dv, lq1, lk1, lq2, lk2, subln_g)


def _moba_select(gate, first, own):
    lane = lax.broadcasted_iota(jnp.int32, gate.shape, 1)
    lane_f = lane.astype(F32)
    neg_inf = -jnp.inf
    cand = jnp.logical_and(lane >= first, lane < first + own)
    gv = jnp.where(cand, gate, neg_inf)
    for _ in range(MOBA_TOPK):
        mx = jnp.max(gv, axis=-1, keepdims=True)
        is_mx = jnp.logical_and(gv == mx, gv > neg_inf)
        idx = jnp.min(jnp.where(is_mx, lane_f, float(LANES)), axis=-1, keepdims=True)
        gv = jnp.where(lane_f == idx, neg_inf, gv)
    return jnp.where(gv > neg_inf, NEG, 0.0)


def _moba_kernel(q_ref, k_ref, v_ref, o_ref, s_ref, p_ref, v_aug_ref, k_aug_ref, q_aug_ref, kmean_ref):
    seq = k_ref.shape[0]
    n_blocks = seq // MOBA_BLOCK
    _fill_v_aug(v_ref, v_aug_ref)

    kf = k_ref[...].astype(F32)
    k_lane = lax.broadcasted_iota(jnp.int32, (seq, LANES), 1)
    k_block = lax.broadcasted_iota(jnp.int32, (seq, LANES), 0) // MOBA_BLOCK
    low = k_lane < MOBA_HEAD_DIM
    k_aug_ref[0] = jnp.where(low, kf, (k_lane - MOBA_HEAD_DIM == k_block).astype(F32)).astype(BF16)
    k_aug_ref[1] = jnp.where(low, (k_lane == k_block).astype(F32), kf).astype(BF16)

    kmean_ref[...] = jnp.zeros(kmean_ref.shape, F32)
    for n in range(n_blocks - 1):
        mean = jnp.sum(kf[_blk(n)], axis=0, keepdims=True) * (1.0 / MOBA_BLOCK)
        kmean_ref[n:n + 1, :] = mean
        kmean_ref[MOBA_HEAD_DIM + n:MOBA_HEAD_DIM + n + 1, :] = mean
    kmean = kmean_ref[...]
    kmean_hi = kmean.astype(BF16)
    kmean_lo = (kmean - kmean_hi.astype(F32)).astype(BF16)

    lane = lax.broadcasted_iota(jnp.int32, (ATT_BLOCK, LANES), 1)
    for own in range(n_blocks):
        qf = q_ref[_blk(own), :].astype(F32)
        q_even = jnp.where(lane < MOBA_HEAD_DIM, qf, 0.0)
        q_odd = jnp.where(lane >= MOBA_HEAD_DIM, qf, 0.0)
        if own > MOBA_TOPK:
            def gate(qh):
                qb = qh.astype(BF16)
                return _dot_nt(qb, kmean_hi) + _dot_nt(qb, kmean_lo)
            q_even = q_even + _moba_select(gate(q_even), MOBA_HEAD_DIM, own)
            q_odd = q_odd + _moba_select(gate(q_odd), 0, own)
        q_aug_ref[0, _blk(own), :] = q_even.astype(BF16)
        q_aug_ref[1, _blk(own), :] = q_odd.astype(BF16)

    for own in reversed(range(n_blocks)):
        def scores(kb):
            return jnp.concatenate([_dot_nt(q_aug_ref[0, _blk(own), :], k_aug_ref[0, _blk(kb), :]),
                                    _dot_nt(q_aug_ref[1, _blk(own), :], k_aug_ref[1, _blk(kb), :])], axis=0)

        buf = own % SCORE_BUFFERS
        out = _attend(own, scores, s_ref.at[buf], p_ref.at[buf], v_aug_ref)
        o_ref[_blk(own), :] = jnp.where(lane < MOBA_HEAD_DIM, out[:ATT_BLOCK], out[ATT_BLOCK:]).astype(BF16)


def _moba_call(mq, mk, mv, batch, seq):
    t = mq.shape[0]
    seq_block = pl.BlockSpec((seq, LANES), lambda b, h: (b, h))
    return pl.pallas_call(
        _moba_kernel,
        grid=(batch, MOBA_HEADS // 2),
        in_specs=[seq_block, seq_block, seq_block],
        out_specs=seq_block,
        out_shape=jax.ShapeDtypeStruct((t, BRANCH_W), BF16),
        scratch_shapes=_att_scratch(seq) + [pltpu.VMEM((2, seq, LANES), BF16),
                                            pltpu.VMEM((2, seq, LANES), BF16),
                                            pltpu.VMEM((LANES, LANES), F32)],
        compiler_params=_params(2),
        name="moba",
    )(mq, mk, mv)


def _merge_kernel(x_ref, fa_ref, fb_ref, fc_ref, wg_ref, bg_ref, wa_ref, wb_ref, wc_ref, wo_ref,
                  g_ref, b_ref, o_ref):
    x = x_ref[...]
    xb = x.astype(BF16)
    m = None
    for j, (f_ref, w_ref) in enumerate(((fa_ref, wa_ref), (fb_ref, wb_ref), (fc_ref, wc_ref))):
        cols = slice(j * D_MODEL, (j + 1) * D_MODEL)
        gate = _sigmoid(_dot(xb, wg_ref[:, cols]) + bg_ref[:, cols])
        term = gate * _dot(f_ref[...], w_ref[...])
        m = term if m is None else m + term
    z = _dot(m.astype(BF16), wo_ref[...])
    o_ref[...] = _layer_norm(DEEPNORM_ALPHA * x + z, g_ref[...], b_ref[...])


def _merge_call(x, fa, fb, fc, w_gate, b_gate, w_a, w_b, w_c, w_o, ln_g, ln_b, tm=256):
    t = x.shape[0]
    row = lambda i: (i, 0)
    full = lambda shape: pl.BlockSpec(shape, lambda i: (0, 0))
    feat = pl.BlockSpec((tm, BRANCH_W), row)
    return pl.pallas_call(
        _merge_kernel,
        grid=(t // tm,),
        in_specs=[pl.BlockSpec((tm, D_MODEL), row), feat, feat, feat,
                  full((D_MODEL, N_BRANCHES * D_MODEL)), full((1, N_BRANCHES * D_MODEL)),
                  full((BRANCH_W, D_MODEL)), full((BRANCH_W, D_MODEL)), full((BRANCH_W, D_MODEL)),
                  full((D_MODEL, D_MODEL)), full((1, D_MODEL)), full((1, D_MODEL))],
        out_specs=pl.BlockSpec((tm, D_MODEL), row),
        out_shape=jax.ShapeDtypeStruct((t, D_MODEL), F32),
        compiler_params=_params(1),
        name="merge",
    )(x, fa, fb, fc, w_gate, b_gate, w_a, w_b, w_c, w_o, ln_g, ln_b)


FF_CHUNK = 1024


def _ffn_kernel(x_ref, w1_ref, b1_ref, w2_ref, b2_ref, g_ref, b_ref, o_ref):
    x = x_ref[...]
    xb = x.astype(BF16)
    y = None
    for c in range(D_FF // FF_CHUNK):
        cols = slice(c * FF_CHUNK, (c + 1) * FF_CHUNK)
        h = jnp.maximum(_dot(xb, w1_ref[:, cols]) + b1_ref[:, cols], 0.0)
        part = _dot((h * h).astype(BF16), w2_ref[cols, :])
        y = part if y is None else y + part
    o_ref[...] = _layer_norm(DEEPNORM_ALPHA * x + y + b2_ref[...], g_ref[...], b_ref[...])


def _ffn_call(x, w1, b1, w2, b2, ln_g, ln_b, tm=256):
    t = x.shape[0]
    row = lambda i: (i, 0)
    full = lambda shape: pl.BlockSpec(shape, lambda i: (0, 0))
    return pl.pallas_call(
        _ffn_kernel,
        grid=(t // tm,),
        in_specs=[pl.BlockSpec((tm, D_MODEL), row),
                  full((D_MODEL, D_FF)), full((1, D_FF)), full((D_FF, D_MODEL)),
                  full((1, D_MODEL)), full((1, D_MODEL)), full((1, D_MODEL))],
        out_specs=pl.BlockSpec((tm, D_MODEL), row),
        out_shape=jax.ShapeDtypeStruct((t, D_MODEL), F32),
        compiler_params=_params(1),
        name="ffn",
    )(x, w1, b1, w2, b2, ln_g, ln_b)


def _rope_tables(seq):
    pos = jnp.arange(seq, dtype=F32)
    inv = ROPE_THETA ** (-jnp.arange(0, DIFF_QK_DIM, 2, dtype=F32) / DIFF_QK_DIM)
    ang = pos[:, None] * inv[None, :]
    ang = jnp.concatenate([ang, ang], axis=-1)
    cos, sin = jnp.cos(ang), jnp.sin(ang)
    first_half = (jnp.arange(DIFF_QK_DIM) < HALF)[None, :]
    sin_a = jnp.where(first_half, -sin, 0.0)
    sin_b = jnp.where(first_half, 0.0, sin)
    tile2 = lambda a: jnp.concatenate([a, a], axis=-1)
    k_tabs = [tile2(cos), tile2(sin_a), tile2(sin_b)]
    scale = DIFF_QK_DIM ** -0.5 * math.log2(math.e)
    q_tabs = [a * scale for a in k_tabs]
    return q_tabs + k_tabs


def kernel(x, w_in, b_gate, conv_w, conv_b, conv_ln_g, conv_ln_b, w_conv_out, lam_q1, lam_k1, lam_q2, lam_k2,
           diff_subln_g, w_diff_out, w_moba_out, w_o, ln1_g, ln1_b, w_ff1, b_ff1, w_ff2, b_ff2, ln2_g, ln2_b):
    batch, seq, d = x.shape
    assert d == D_MODEL and seq % 512 == 0
    tables = _rope_tables(seq)
    row = lambda a: a.reshape(1, -1)
    xs = x.reshape(batch * seq, d)
    for l in range(DEPTH):
        w_p = w_in[l, :, :PROJ_COLS].astype(BF16)
        w_g = w_in[l, :, PROJ_COLS:].astype(BF16)
        h, dq, dk, dv, mq, mk, mv = _proj_call(xs, w_p, tables, seq)
        fa = _conv_call(h, conv_w[l, :, 0, :], row(conv_b[l]), row(conv_ln_g[l]), row(conv_ln_b[l]), batch, seq)
        fb = _diff_call(dq, dk, dv, row(lam_q1[l]), row(lam_k1[l]), row(lam_q2[l]), row(lam_k2[l]),
                        row(diff_subln_g[l]), l, batch, seq)
        fc = _moba_call(mq, mk, mv, batch, seq)
        xs = _merge_call(xs, fa, fb, fc, w_g, row(b_gate[l]),
                         w_conv_out[l].astype(BF16), w_diff_out[l].astype(BF16), w_moba_out[l].astype(BF16),
                         w_o[l].astype(BF16), row(ln1_g[l]), row(ln1_b[l]))
        xs = _ffn_call(xs, w_ff1[l].astype(BF16), row(b_ff1[l]), w_ff2[l].astype(BF16), row(b_ff2[l]),
                       row(ln2_g[l]), row(ln2_b[l]))
    return xs.reshape(batch, seq, d)
```

```python
import functools
import math

import jax
import jax.numpy as jnp
from jax import lax
from jax.experimental import pallas as pl
from jax.experimental.pallas import tpu as pltpu

F32 = jnp.float32
BF16 = jnp.bfloat16

D_MODEL = 1024
DEPTH = 4
CONV_CH = D_MODEL // 2
CONV_WIDTH = 31
DIFF_QK_DIM = 64
DIFF_V_DIM = 2 * DIFF_QK_DIM
DIFF_HEADS = D_MODEL // 256
MOBA_HEAD_DIM = 64
MOBA_HEADS = D_MODEL // 128
MOBA_BLOCK = 256
MOBA_TOPK = 3
D_FF = 4 * D_MODEL
ROPE_THETA = 10000.0
N_BRANCHES = 3
LN_EPS = 1e-5
NEG = -1e30
DEEPNORM_ALPHA = (2.0 * DEPTH) ** 0.25

BRANCH_W = 512
PROJ_COLS = 8 * BRANCH_W
LANES = 128
SUBLANES = 8
HALF = DIFF_QK_DIM // 2
ATT_BLOCK = 256
SCORE_BUFFERS = 2
ROW_TILE = 1024
SUB_TILE = 256
CONV_TILE = 256
CONV_HALO = 32
VMEM_LIMIT = 56 * 1024 * 1024


def _params(n_axes):
    return pltpu.CompilerParams(dimension_semantics=("arbitrary",) * n_axes,
                                vmem_limit_bytes=VMEM_LIMIT)


def _layer_weight(shape, layer, col_block=0):
    return pl.BlockSpec((None,) + shape, lambda i: (layer, 0, col_block), pipeline_mode=pl.Buffered(1))


def _sub(s):
    return slice(s * SUB_TILE, (s + 1) * SUB_TILE)


def _sigmoid(x):
    return 1.0 / (1.0 + jnp.exp(-x))


def _layer_norm(y, g, b):
    mu = jnp.mean(y, axis=-1, keepdims=True)
    d = y - mu
    var = jnp.mean(d * d, axis=-1, keepdims=True)
    return d * lax.rsqrt(var + LN_EPS) * g + b


def _dot(a, b):
    return jnp.dot(a, b, preferred_element_type=F32)


def _dot_nt(a, b):
    return lax.dot_general(a, b, (((1,), (1,)), ((), ())), preferred_element_type=F32)


def _rope(u, c, sa, sb):
    outs = []
    for j in range(u.shape[1] // LANES):
        uc = u[:, LANES * j:LANES * (j + 1)]
        fwd = pltpu.roll(uc, HALF, 1)
        bwd = pltpu.roll(uc, LANES - HALF, 1)
        outs.append(uc * c + bwd * sa + fwd * sb)
    return jnp.concatenate(outs, axis=1)


def _proj_kernel(x_ref, w_ref, cq_ref, sqa_ref, sqb_ref, ck_ref, ska_ref, skb_ref,
                 h_ref, dq_ref, dk_ref, dv_ref, mq_ref, mk_ref, mv_ref):
    xb = x_ref[...].astype(BF16)

    def mm(group):
        return _dot(xb, w_ref[:, group * BRANCH_W:(group + 1) * BRANCH_W])

    h_ref[...] = mm(0) * _sigmoid(mm(1))
    cq, sqa, sqb = cq_ref[...], sqa_ref[...], sqb_ref[...]
    ck, ska, skb = ck_ref[...], ska_ref[...], skb_ref[...]
    dq_ref[...] = _rope(mm(2), cq, sqa, sqb).astype(BF16)
    dk_ref[...] = _rope(mm(3), ck, ska, skb).astype(BF16)
    dv_ref[...] = mm(4).astype(BF16)
    mq_ref[...] = _rope(mm(5), cq, sqa, sqb).astype(BF16)
    mk_ref[...] = _rope(mm(6), ck, ska, skb).astype(BF16)
    mv_ref[...] = mm(7).astype(BF16)


def _proj_call(x, w_in_b, tables, seq, layer, tm=512):
    t = x.shape[0]
    n_seq_tiles = seq // tm
    row = lambda i: (i, 0)
    tab = lambda i: (i % n_seq_tiles, 0)
    out_spec = pl.BlockSpec((tm, BRANCH_W), row)
    return pl.pallas_call(
        _proj_kernel,
        grid=(t // tm,),
        in_specs=[pl.BlockSpec((tm, D_MODEL), row), _layer_weight((D_MODEL, PROJ_COLS), layer)]
                 + [pl.BlockSpec((tm, LANES), tab)] * 6,
        out_specs=[out_spec] * 7,
        out_shape=[jax.ShapeDtypeStruct((t, BRANCH_W), F32)]
                  + [jax.ShapeDtypeStruct((t, BRANCH_W), BF16)] * 6,
        compiler_params=_params(1),
        name="proj",
    )(x, w_in_b, *tables)


def _conv_kernel(prev_ref, cur_ref, w_ref, b_ref, g_ref, beta_ref, o_ref, pad_ref):
    i = pl.program_id(1)
    keep = (i > 0).astype(F32)
    pad_ref[0:CONV_HALO, :] = prev_ref[...] * keep
    pad_ref[CONV_HALO:CONV_HALO + CONV_TILE, :] = cur_ref[...]
    base = CONV_HALO - (CONV_WIDTH - 1)
    acc = jnp.zeros((CONV_TILE, CONV_CH), F32) + b_ref[...]
    for r in range(SUBLANES):
        taps = [k for k in range(CONV_WIDTH) if (base + k) % SUBLANES == r]
        last = base + taps[-1]
        slab = pad_ref[r:last + CONV_TILE, :]
        part = None
        for k in taps:
            off = base + k - r
            term = slab[off:off + CONV_TILE, :] * w_ref[k:k + 1, :]
            part = term if part is None else part + term
        acc = acc + part
    y = _layer_norm(acc, g_ref[...], beta_ref[...])
    o_ref[...] = (y * _sigmoid(y)).astype(BF16)


def _conv_call(h, conv_w, conv_b, ln_g, ln_b, batch, seq):
    t = h.shape[0]
    n_tiles = seq // CONV_TILE
    halo_per_tile = CONV_TILE // CONV_HALO
    cur = lambda b, i: (b * n_tiles + i, 0)
    prev = lambda b, i: (jnp.maximum((b * n_tiles + i) * halo_per_tile - 1, 0), 0)
    vec = pl.BlockSpec((1, CONV_CH), lambda b, i: (0, 0))
    return pl.pallas_call(
        _conv_kernel,
        grid=(batch, n_tiles),
        in_specs=[pl.BlockSpec((CONV_HALO, CONV_CH), prev),
                  pl.BlockSpec((CONV_TILE, CONV_CH), cur),
                  pl.BlockSpec((CONV_WIDTH, CONV_CH), lambda b, i: (0, 0)),
                  vec, vec, vec],
        out_specs=pl.BlockSpec((CONV_TILE, CONV_CH), cur),
        out_shape=jax.ShapeDtypeStruct((t, CONV_CH), BF16),
        scratch_shapes=[pltpu.VMEM((CONV_HALO + CONV_TILE, CONV_CH), F32)],
        compiler_params=_params(2),
        name="conv",
    )(h, h, conv_w, conv_b, ln_g, ln_b)


MAPS = 2
ROWS = MAPS * ATT_BLOCK


def _blk(kb):
    return slice(kb * ATT_BLOCK, (kb + 1) * ATT_BLOCK)


def _causal_mask():
    r = lax.broadcasted_iota(jnp.int32, (ROWS, ATT_BLOCK), 0) % ATT_BLOCK
    c = lax.broadcasted_iota(jnp.int32, (ROWS, ATT_BLOCK), 1)
    return c <= r


def _fill_v_aug(v_ref, v_aug_ref):
    v_aug_ref[:, :LANES] = v_ref[...]
    v_aug_ref[:, LANES:] = jnp.ones((v_ref.shape[0], LANES), BF16)


def _attend(tile, score_fn, s_ref, p_ref, v_aug_ref):
    n_kb = tile + 1
    m_part = None
    for kb in range(n_kb):
        s = score_fn(kb)
        if kb == tile:
            s = jnp.where(_causal_mask(), s, NEG)
        s_ref[:, _blk(kb)] = s
        folded = jnp.maximum(s[:, :LANES], s[:, LANES:])
        m_part = folded if m_part is None else jnp.maximum(m_part, folded)
    m = jnp.max(m_part, axis=-1, keepdims=True)
    for kb in range(n_kb):
        p_ref[:, _blk(kb)] = jnp.exp2(s_ref[:, _blk(kb)] - m).astype(BF16)
    n_keys = n_kb * ATT_BLOCK
    acc = _dot(p_ref[:, :n_keys], v_aug_ref[:n_keys, :])
    return acc[:, :LANES] / acc[:, LANES:]


def _att_scratch(seq):
    return [pltpu.VMEM((SCORE_BUFFERS, ROWS, seq), F32),
            pltpu.VMEM((SCORE_BUFFERS, ROWS, seq), BF16),
            pltpu.VMEM((seq, 2 * LANES), BF16)]


def _diff_kernel(q_ref, k_ref, v_ref, lq1_ref, lk1_ref, lq2_ref, lk2_ref, g_ref, o_ref,
                 s_ref, p_ref, v_aug_ref, *, lambda_init):
    lam = (jnp.exp(jnp.sum(lq1_ref[...] * lk1_ref[...], axis=-1, keepdims=True))
           - jnp.exp(jnp.sum(lq2_ref[...] * lk2_ref[...], axis=-1, keepdims=True)) + lambda_init)
    _fill_v_aug(v_ref, v_aug_ref)
    lane = lax.broadcasted_iota(jnp.int32, (ATT_BLOCK, LANES), 1)
    for tile in reversed(range(k_ref.shape[0] // ATT_BLOCK)):
        qf = q_ref[_blk(tile), :].astype(F32)
        qq = jnp.concatenate([jnp.where(lane < DIFF_QK_DIM, qf, 0.0),
                              jnp.where(lane >= DIFF_QK_DIM, qf, 0.0)], axis=0).astype(BF16)
        buf = tile % SCORE_BUFFERS
        out = _attend(tile, lambda kb: _dot_nt(qq, k_ref[_blk(kb), :]), s_ref.at[buf], p_ref.at[buf], v_aug_ref)
        o = out[:ATT_BLOCK] - lam * out[ATT_BLOCK:]
        o = o * lax.rsqrt(jnp.mean(o * o, axis=-1, keepdims=True) + LN_EPS) * g_ref[...]
        o_ref[_blk(tile), :] = (o * (1.0 - lambda_init)).astype(BF16)


def _diff_call(dq, dk, dv, lq1, lk1, lq2, lk2, subln_g, layer_idx, batch, seq):
    t = dq.shape[0]
    lambda_init = 0.8 - 0.6 * math.exp(-0.3 * layer_idx)
    lam_spec = pl.BlockSpec((1, DIFF_QK_DIM), lambda b, h: (0, 0))
    seq_block = pl.BlockSpec((seq, LANES), lambda b, h: (b, h))
    return pl.pallas_call(
        functools.partial(_diff_kernel, lambda_init=lambda_init),
        grid=(batch, DIFF_HEADS),
        in_specs=[seq_block, seq_block, seq_block,
                  lam_spec, lam_spec, lam_spec, lam_spec,
                  pl.BlockSpec((1, DIFF_V_DIM), lambda b, h: (0, 0))],
        out_specs=seq_block,
        out_shape=jax.ShapeDtypeStruct((t, BRANCH_W), BF16),
        scratch_shapes=_att_scratch(seq),
        compiler_params=_params(2),
        name="diff_attn",
    )(dq, dk, dv, lq1, lk1, lq2, lk2, subln_g)


def _moba_select(gate, first, own):
    lane = lax.broadcasted_iota(jnp.int32, gate.shape, 1)
    lane_f = lane.astype(F32)
    neg_inf = -jnp.inf
    cand = jnp.logical_and(lane >= first, lane < first + own)
    gv = jnp.where(cand, gate, neg_inf)
    for _ in range(MOBA_TOPK):
        mx = jnp.max(gv, axis=-1, keepdims=True)
        is_mx = jnp.logical_and(gv == mx, gv > neg_inf)
        idx = jnp.min(jnp.where(is_mx, lane_f, float(LANES)), axis=-1, keepdims=True)
        gv = jnp.where(lane_f == idx, neg_inf, gv)
    return jnp.where(gv > neg_inf, NEG, 0.0)


def _moba_kernel(q_ref, k_ref, v_ref, o_ref, s_ref, p_ref, v_aug_ref, k_aug_ref, q_aug_ref, kmean_ref):
    seq = k_ref.shape[0]
    n_blocks = seq // MOBA_BLOCK
    _fill_v_aug(v_ref, v_aug_ref)

    kf = k_ref[...].astype(F32)
    k_lane = lax.broadcasted_iota(jnp.int32, (seq, LANES), 1)
    k_block = lax.broadcasted_iota(jnp.int32, (seq, LANES), 0) // MOBA_BLOCK
    low = k_lane < MOBA_HEAD_DIM
    k_aug_ref[0] = jnp.where(low, kf, (k_lane - MOBA_HEAD_DIM == k_block).astype(F32)).astype(BF16)
    k_aug_ref[1] = jnp.where(low, (k_lane == k_block).astype(F32), kf).astype(BF16)

    kmean_ref[...] = jnp.zeros(kmean_ref.shape, F32)
    for n in range(n_blocks - 1):
        mean = jnp.sum(kf[_blk(n)], axis=0, keepdims=True) * (1.0 / MOBA_BLOCK)
        kmean_ref[n:n + 1, :] = mean
        kmean_ref[MOBA_HEAD_DIM + n:MOBA_HEAD_DIM + n + 1, :] = mean
    kmean = kmean_ref[...]
    kmean_hi = kmean.astype(BF16)
    kmean_lo = (kmean - kmean_hi.astype(F32)).astype(BF16)

    lane = lax.broadcasted_iota(jnp.int32, (ATT_BLOCK, LANES), 1)
    for own in range(n_blocks):
        qf = q_ref[_blk(own), :].astype(F32)
        q_even = jnp.where(lane < MOBA_HEAD_DIM, qf, 0.0)
        q_odd = jnp.where(lane >= MOBA_HEAD_DIM, qf, 0.0)
        if own > MOBA_TOPK:
            def gate(qh):
                qb = qh.astype(BF16)
                return _dot_nt(qb, kmean_hi) + _dot_nt(qb, kmean_lo)
            q_even = q_even + _moba_select(gate(q_even), MOBA_HEAD_DIM, own)
            q_odd = q_odd + _moba_select(gate(q_odd), 0, own)
        q_aug_ref[0, _blk(own), :] = q_even.astype(BF16)
        q_aug_ref[1, _blk(own), :] = q_odd.astype(BF16)

    for own in reversed(range(n_blocks)):
        def scores(kb):
            return jnp.concatenate([_dot_nt(q_aug_ref[0, _blk(own), :], k_aug_ref[0, _blk(kb), :]),
                                    _dot_nt(q_aug_ref[1, _blk(own), :], k_aug_ref[1, _blk(kb), :])], axis=0)

        buf = own % SCORE_BUFFERS
        out = _attend(own, scores, s_ref.at[buf], p_ref.at[buf], v_aug_ref)
        o_ref[_blk(own), :] = jnp.where(lane < MOBA_HEAD_DIM, out[:ATT_BLOCK], out[ATT_BLOCK:]).astype(BF16)


def _moba_call(mq, mk, mv, batch, seq):
    t = mq.shape[0]
    seq_block = pl.BlockSpec((seq, LANES), lambda b, h: (b, h))
    return pl.pallas_call(
        _moba_kernel,
        grid=(batch, MOBA_HEADS // 2),
        in_specs=[seq_block, seq_block, seq_block],
        out_specs=seq_block,
        out_shape=jax.ShapeDtypeStruct((t, BRANCH_W), BF16),
        scratch_shapes=_att_scratch(seq) + [pltpu.VMEM((2, seq, LANES), BF16),
                                            pltpu.VMEM((2, seq, LANES), BF16),
                                            pltpu.VMEM((LANES, LANES), F32)],
        compiler_params=_params(2),
        name="moba",
    )(mq, mk, mv)


def _merge_kernel(x_ref, fa_ref, fb_ref, fc_ref, wg0_ref, wg1_ref, wg2_ref, bg_ref, wa_ref, wb_ref, wc_ref,
                  wo_ref, g_ref, b_ref, o_ref):
    for s in range(x_ref.shape[0] // SUB_TILE):
        x = x_ref[_sub(s), :]
        xb = x.astype(BF16)
        m = None
        for j, (f_ref, wg_ref, w_ref) in enumerate(((fa_ref, wg0_ref, wa_ref), (fb_ref, wg1_ref, wb_ref),
                                                   (fc_ref, wg2_ref, wc_ref))):
            gate = _sigmoid(_dot(xb, wg_ref[...]) + bg_ref[:, j * D_MODEL:(j + 1) * D_MODEL])
            term = gate * _dot(f_ref[_sub(s), :], w_ref[...])
            m = term if m is None else m + term
        z = _dot(m.astype(BF16), wo_ref[...])
        o_ref[_sub(s), :] = _layer_norm(DEEPNORM_ALPHA * x + z, g_ref[...], b_ref[...])


def _merge_call(x, fa, fb, fc, w_in_b, b_gate, w_a, w_b, w_c, w_o, ln_g, ln_b, layer, tm=ROW_TILE):
    t = x.shape[0]
    row = lambda i: (i, 0)
    vec = lambda n: pl.BlockSpec((1, n), lambda i: (0, 0))
    feat = pl.BlockSpec((tm, BRANCH_W), row)
    gate_cols = [_layer_weight((D_MODEL, D_MODEL), layer, PROJ_COLS // D_MODEL + j) for j in range(N_BRANCHES)]
    branch_w = _layer_weight((BRANCH_W, D_MODEL), layer)
    return pl.pallas_call(
        _merge_kernel,
        grid=(t // tm,),
        in_specs=[pl.BlockSpec((tm, D_MODEL), row), feat, feat, feat, *gate_cols, vec(N_BRANCHES * D_MODEL),
                  branch_w, branch_w, branch_w, _layer_weight((D_MODEL, D_MODEL), layer),
                  vec(D_MODEL), vec(D_MODEL)],
        out_specs=pl.BlockSpec((tm, D_MODEL), row),
        out_shape=jax.ShapeDtypeStruct((t, D_MODEL), F32),
        compiler_params=_params(1),
        name="merge",
    )(x, fa, fb, fc, w_in_b, w_in_b, w_in_b, b_gate, w_a, w_b, w_c, w_o, ln_g, ln_b)


FF_CHUNK = 1024


def _ffn_kernel(x_ref, w1_ref, b1_ref, w2_ref, b2_ref, g_ref, b_ref, o_ref):
    for s in range(x_ref.shape[0] // SUB_TILE):
        x = x_ref[_sub(s), :]
        xb = x.astype(BF16)
        y = None
        for c in range(D_FF // FF_CHUNK):
            cols = slice(c * FF_CHUNK, (c + 1) * FF_CHUNK)
            h = jnp.maximum(_dot(xb, w1_ref[:, cols]) + b1_ref[:, cols], 0.0)
            part = _dot((h * h).astype(BF16), w2_ref[cols, :])
            y = part if y is None else y + part
        o_ref[_sub(s), :] = _layer_norm(DEEPNORM_ALPHA * x + y + b2_ref[...], g_ref[...], b_ref[...])


def _ffn_call(x, w1, b1, w2, b2, ln_g, ln_b, layer, tm=ROW_TILE):
    t = x.shape[0]
    row = lambda i: (i, 0)
    vec = lambda n: pl.BlockSpec((1, n), lambda i: (0, 0))
    return pl.pallas_call(
        _ffn_kernel,
        grid=(t // tm,),
        in_specs=[pl.BlockSpec((tm, D_MODEL), row),
                  _layer_weight((D_MODEL, D_FF), layer), vec(D_FF), _layer_weight((D_FF, D_MODEL), layer),
                  vec(D_MODEL), vec(D_MODEL), vec(D_MODEL)],
        out_specs=pl.BlockSpec((tm, D_MODEL), row),
        out_shape=jax.ShapeDtypeStruct((t, D_MODEL), F32),
        compiler_params=_params(1),
        name="ffn",
    )(x, w1, b1, w2, b2, ln_g, ln_b)


def _rope_tables(seq):
    pos = jnp.arange(seq, dtype=F32)
    inv = ROPE_THETA ** (-jnp.arange(0, DIFF_QK_DIM, 2, dtype=F32) / DIFF_QK_DIM)
    ang = pos[:, None] * inv[None, :]
    ang = jnp.concatenate([ang, ang], axis=-1)
    cos, sin = jnp.cos(ang), jnp.sin(ang)
    first_half = (jnp.arange(DIFF_QK_DIM) < HALF)[None, :]
    sin_a = jnp.where(first_half, -sin, 0.0)
    sin_b = jnp.where(first_half, 0.0, sin)
    tile2 = lambda a: jnp.concatenate([a, a], axis=-1)
    k_tabs = [tile2(cos), tile2(sin_a), tile2(sin_b)]
    scale = DIFF_QK_DIM ** -0.5 * math.log2(math.e)
    q_tabs = [a * scale for a in k_tabs]
    return q_tabs + k_tabs


def kernel(x, w_in, b_gate, conv_w, conv_b, conv_ln_g, conv_ln_b, w_conv_out, lam_q1, lam_k1, lam_q2, lam_k2,
           diff_subln_g, w_diff_out, w_moba_out, w_o, ln1_g, ln1_b, w_ff1, b_ff1, w_ff2, b_ff2, ln2_g, ln2_b):
    batch, seq, d = x.shape
    assert d == D_MODEL and seq % 512 == 0
    tables = _rope_tables(seq)
    row = lambda a: a.reshape(1, -1)
    xs = x.reshape(batch * seq, d)
    w_in_b, w_a, w_b, w_c, w_o_b, w1, w2 = (w.astype(BF16) for w in
                                            (w_in, w_conv_out, w_diff_out, w_moba_out, w_o, w_ff1, w_ff2))
    for l in range(DEPTH):
        h, dq, dk, dv, mq, mk, mv = _proj_call(xs, w_in_b, tables, seq, l)
        fa = _conv_call(h, conv_w[l, :, 0, :], row(conv_b[l]), row(conv_ln_g[l]), row(conv_ln_b[l]), batch, seq)
        fb = _diff_call(dq, dk, dv, row(lam_q1[l]), row(lam_k1[l]), row(lam_q2[l]), row(lam_k2[l]),
                        row(diff_subln_g[l]), l, batch, seq)
        fc = _moba_call(mq, mk, mv, batch, seq)
        xs = _merge_call(xs, fa, fb, fc, w_in_b, row(b_gate[l]), w_a, w_b, w_c, w_o_b,
                         row(ln1_g[l]), row(ln1_b[l]), l)
        xs = _ffn_call(xs, w1, row(b_ff1[l]), w2, row(b_ff2[l]), row(ln2_g[l]), row(ln2_b[l]), l)
    return xs.reshape(batch, seq, d)
```

```python
import functools
import math

import jax
import jax.numpy as jnp
from jax import lax
from jax.experimental import pallas as pl
from jax.experimental.pallas import tpu as pltpu

F32 = jnp.float32
BF16 = jnp.bfloat16

D_MODEL = 1024
DEPTH = 4
CONV_CH = D_MODEL // 2
CONV_WIDTH = 31
DIFF_QK_DIM = 64
DIFF_V_DIM = 2 * DIFF_QK_DIM
DIFF_HEADS = D_MODEL // 256
MOBA_HEAD_DIM = 64
MOBA_HEADS = D_MODEL // 128
MOBA_BLOCK = 256
MOBA_TOPK = 3
D_FF = 4 * D_MODEL
ROPE_THETA = 10000.0
N_BRANCHES = 3
LN_EPS = 1e-5
NEG = -1e30
DEEPNORM_ALPHA = (2.0 * DEPTH) ** 0.25

BRANCH_W = 512
PROJ_COLS = 8 * BRANCH_W
LANES = 128
SUBLANES = 8
HALF = DIFF_QK_DIM // 2
ATT_BLOCK = 256
SCORE_AHEAD = 3
SCORE_BUFFERS = SCORE_AHEAD + 1
ROW_TILE = 1024
SUB_TILE = 256
PROJ_SUB_TILE = 512
CONV_TILE = 1024
CONV_SUB_TILE = 256
CONV_HALO = 32
VMEM_LIMIT = 56 * 1024 * 1024


def _params(n_axes):
    return pltpu.CompilerParams(dimension_semantics=("arbitrary",) * n_axes,
                                vmem_limit_bytes=VMEM_LIMIT)


def _layer_weight(shape, layer, col_block=0):
    return pl.BlockSpec((None,) + shape, lambda i: (layer, 0, col_block), pipeline_mode=pl.Buffered(1))


def _sub(s):
    return slice(s * SUB_TILE, (s + 1) * SUB_TILE)


def _sigmoid(x):
    return 1.0 / (1.0 + jnp.exp(-x))


def _layer_norm(y, g, b):
    mu = jnp.mean(y, axis=-1, keepdims=True)
    d = y - mu
    var = jnp.mean(d * d, axis=-1, keepdims=True)
    return d * lax.rsqrt(var + LN_EPS) * g + b


def _dot(a, b):
    return jnp.dot(a, b, preferred_element_type=F32)


def _dot_nt(a, b):
    return lax.dot_general(a, b, (((1,), (1,)), ((), ())), preferred_element_type=F32)


def _rope(u, c, sa, sb):
    outs = []
    for j in range(u.shape[1] // LANES):
        uc = u[:, LANES * j:LANES * (j + 1)]
        fwd = pltpu.roll(uc, HALF, 1)
        bwd = pltpu.roll(uc, LANES - HALF, 1)
        outs.append(uc * c + bwd * sa + fwd * sb)
    return jnp.concatenate(outs, axis=1)


def _proj_kernel(x_ref, w_ref, cq_ref, sqa_ref, sqb_ref, ck_ref, ska_ref, skb_ref,
                 h_ref, dq_ref, dk_ref, dv_ref, mq_ref, mk_ref, mv_ref):
    for s in range(x_ref.shape[0] // PROJ_SUB_TILE):
        rows = slice(s * PROJ_SUB_TILE, (s + 1) * PROJ_SUB_TILE)
        xb = x_ref[rows, :].astype(BF16)

        def mm(group):
            return _dot(xb, w_ref[:, group * BRANCH_W:(group + 1) * BRANCH_W])

        h_ref[rows, :] = mm(0) * _sigmoid(mm(1))
        cq, sqa, sqb = cq_ref[rows, :], sqa_ref[rows, :], sqb_ref[rows, :]
        ck, ska, skb = ck_ref[rows, :], ska_ref[rows, :], skb_ref[rows, :]
        dq_ref[rows, :] = _rope(mm(2), cq, sqa, sqb).astype(BF16)
        dk_ref[rows, :] = _rope(mm(3), ck, ska, skb).astype(BF16)
        dv_ref[rows, :] = mm(4).astype(BF16)
        mq_ref[rows, :] = _rope(mm(5), cq, sqa, sqb).astype(BF16)
        mk_ref[rows, :] = _rope(mm(6), ck, ska, skb).astype(BF16)
        mv_ref[rows, :] = mm(7).astype(BF16)


def _proj_call(x, w_in_b, tables, seq, layer, tm=ROW_TILE):
    t = x.shape[0]
    n_seq_tiles = seq // tm
    row = lambda i: (i, 0)
    tab = lambda i: (i % n_seq_tiles, 0)
    out_spec = pl.BlockSpec((tm, BRANCH_W), row)
    return pl.pallas_call(
        _proj_kernel,
        grid=(t // tm,),
        in_specs=[pl.BlockSpec((tm, D_MODEL), row), _layer_weight((D_MODEL, PROJ_COLS), layer)]
                 + [pl.BlockSpec((tm, LANES), tab)] * 6,
        out_specs=[out_spec] * 7,
        out_shape=[jax.ShapeDtypeStruct((t, BRANCH_W), F32)]
                  + [jax.ShapeDtypeStruct((t, BRANCH_W), BF16)] * 6,
        compiler_params=_params(1),
        name="proj",
    )(x, w_in_b, *tables)


def _conv_kernel(prev_ref, cur_ref, w_ref, b_ref, g_ref, beta_ref, o_ref, pad_ref):
    i = pl.program_id(1)
    keep = (i > 0).astype(F32)
    pad_ref[0:CONV_HALO, :] = prev_ref[...] * keep
    pad_ref[CONV_HALO:CONV_HALO + CONV_TILE, :] = cur_ref[...]
    base = CONV_HALO - (CONV_WIDTH - 1)
    for s in range(CONV_TILE // CONV_SUB_TILE):
        first = s * CONV_SUB_TILE
        acc = jnp.zeros((CONV_SUB_TILE, CONV_CH), F32) + b_ref[...]
        for r in range(SUBLANES):
            taps = [k for k in range(CONV_WIDTH) if (base + k) % SUBLANES == r]
            last = base + taps[-1]
            slab = pad_ref[first + r:first + last + CONV_SUB_TILE, :]
            part = None
            for k in taps:
                off = base + k - r
                term = slab[off:off + CONV_SUB_TILE, :] * w_ref[k:k + 1, :]
                part = term if part is None else part + term
            acc = acc + part
        y = _layer_norm(acc, g_ref[...], beta_ref[...])
        o_ref[first:first + CONV_SUB_TILE, :] = (y * _sigmoid(y)).astype(BF16)


def _conv_call(h, conv_w, conv_b, ln_g, ln_b, batch, seq):
    t = h.shape[0]
    n_tiles = seq // CONV_TILE
    halo_per_tile = CONV_TILE // CONV_HALO
    cur = lambda b, i: (b * n_tiles + i, 0)
    prev = lambda b, i: (jnp.maximum((b * n_tiles + i) * halo_per_tile - 1, 0), 0)
    vec = pl.BlockSpec((1, CONV_CH), lambda b, i: (0, 0))
    return pl.pallas_call(
        _conv_kernel,
        grid=(batch, n_tiles),
        in_specs=[pl.BlockSpec((CONV_HALO, CONV_CH), prev),
                  pl.BlockSpec((CONV_TILE, CONV_CH), cur),
                  pl.BlockSpec((CONV_WIDTH, CONV_CH), lambda b, i: (0, 0)),
                  vec, vec, vec],
        out_specs=pl.BlockSpec((CONV_TILE, CONV_CH), cur),
        out_shape=jax.ShapeDtypeStruct((t, CONV_CH), BF16),
        scratch_shapes=[pltpu.VMEM((CONV_HALO + CONV_TILE, CONV_CH), F32)],
        compiler_params=_params(2),
        name="conv",
    )(h, h, conv_w, conv_b, ln_g, ln_b)


MAPS = 2
ROWS = MAPS * ATT_BLOCK


def _blk(kb):
    return slice(kb * ATT_BLOCK, (kb + 1) * ATT_BLOCK)


def _causal_mask():
    r = lax.broadcasted_iota(jnp.int32, (ROWS, ATT_BLOCK), 0) % ATT_BLOCK
    c = lax.broadcasted_iota(jnp.int32, (ROWS, ATT_BLOCK), 1)
    return c <= r


def _fill_v_aug(v_ref, v_aug_ref):
    v_aug_ref[:, :LANES] = v_ref[...]
    v_aug_ref[:, LANES:] = jnp.ones((v_ref.shape[0], LANES), BF16)


def _tile_scores(tile, score_fn, s_ref):
    m_part = None
    for kb in range(tile + 1):
        s = score_fn(kb)
        if kb == tile:
            s = jnp.where(_causal_mask(), s, NEG)
        s_ref[:, _blk(kb)] = s
        folded = jnp.maximum(s[:, :LANES], s[:, LANES:])
        m_part = folded if m_part is None else jnp.maximum(m_part, folded)
    return jnp.max(m_part, axis=-1, keepdims=True)


def _pipelined_tiles(order, scores, s_ref, p_ref, v_aug_ref, finish, after_prologue=None):
    buf = {tile: pos % SCORE_BUFFERS for pos, tile in enumerate(order)}
    maxima = {tile: scores(tile, s_ref.at[buf[tile]]) for tile in order[:SCORE_AHEAD]}
    if after_prologue is not None:
        after_prologue()
    for pos, tile in enumerate(order):
        m, b = maxima.pop(tile), buf[tile]
        for kb in range(tile + 1):
            p_ref[b, :, _blk(kb)] = jnp.exp2(s_ref[b, :, _blk(kb)] - m).astype(BF16)
        if pos + SCORE_AHEAD < len(order):
            ahead = order[pos + SCORE_AHEAD]
            maxima[ahead] = scores(ahead, s_ref.at[buf[ahead]])
        n_keys = (tile + 1) * ATT_BLOCK
        acc = _dot(p_ref[b, :, :n_keys], v_aug_ref[:n_keys, :])
        finish(tile, acc[:, :LANES] / acc[:, LANES:])


def _att_scratch(seq):
    return [pltpu.VMEM((SCORE_BUFFERS, ROWS, seq), F32),
            pltpu.VMEM((SCORE_BUFFERS, ROWS, seq), BF16),
            pltpu.VMEM((seq, 2 * LANES), BF16)]


def _diff_kernel(q_ref, k_ref, v_ref, lq1_ref, lk1_ref, lq2_ref, lk2_ref, g_ref, o_ref,
                 s_ref, p_ref, v_aug_ref, *, lambda_init):
    lam = (jnp.exp(jnp.sum(lq1_ref[...] * lk1_ref[...], axis=-1, keepdims=True))
           - jnp.exp(jnp.sum(lq2_ref[...] * lk2_ref[...], axis=-1, keepdims=True)) + lambda_init)
    _fill_v_aug(v_ref, v_aug_ref)
    lane = lax.broadcasted_iota(jnp.int32, (ATT_BLOCK, LANES), 1)

    def scores(tile, s_buf):
        qf = q_ref[_blk(tile), :].astype(F32)
        qq = jnp.concatenate([jnp.where(lane < DIFF_QK_DIM, qf, 0.0),
                              jnp.where(lane >= DIFF_QK_DIM, qf, 0.0)], axis=0).astype(BF16)
        return _tile_scores(tile, lambda kb: _dot_nt(qq, k_ref[_blk(kb), :]), s_buf)

    def finish(tile, out):
        o = out[:ATT_BLOCK] - lam * out[ATT_BLOCK:]
        o = o * lax.rsqrt(jnp.mean(o * o, axis=-1, keepdims=True) + LN_EPS) * g_ref[...]
        o_ref[_blk(tile), :] = (o * (1.0 - lambda_init)).astype(BF16)

    _pipelined_tiles(list(reversed(range(k_ref.shape[0] // ATT_BLOCK))), scores, s_ref, p_ref, v_aug_ref, finish)


def _diff_call(dq, dk, dv, lq1, lk1, lq2, lk2, subln_g, layer_idx, batch, seq):
    t = dq.shape[0]
    lambda_init = 0.8 - 0.6 * math.exp(-0.3 * layer_idx)
    lam_spec = pl.BlockSpec((1, DIFF_QK_DIM), lambda b, h: (0, 0))
    seq_block = pl.BlockSpec((seq, LANES), lambda b, h: (b, h))
    return pl.pallas_call(
        functools.partial(_diff_kernel, lambda_init=lambda_init),
        grid=(batch, DIFF_HEADS),
        in_specs=[seq_block, seq_block, seq_block,
                  lam_spec, lam_spec, lam_spec, lam_spec,
                  pl.BlockSpec((1, DIFF_V_DIM), lambda b, h: (0, 0))],
        out_specs=seq_block,
        out_shape=jax.ShapeDtypeStruct((t, BRANCH_W), BF16),
        scratch_shapes=_att_scratch(seq),
        compiler_params=_params(2),
        name="diff_attn",
    )(dq, dk, dv, lq1, lk1, lq2, lk2, subln_g)


def _moba_select(gate, first, own):
    lane = lax.broadcasted_iota(jnp.int32, gate.shape, 1)
    lane_f = lane.astype(F32)
    neg_inf = -jnp.inf
    cand = jnp.logical_and(lane >= first, lane < first + own)
    gv = jnp.where(cand, gate, neg_inf)
    for _ in range(MOBA_TOPK):
        mx = jnp.max(gv, axis=-1, keepdims=True)
        is_mx = jnp.logical_and(gv == mx, gv > neg_inf)
        idx = jnp.min(jnp.where(is_mx, lane_f, float(LANES)), axis=-1, keepdims=True)
        gv = jnp.where(lane_f == idx, neg_inf, gv)
    return jnp.where(gv > neg_inf, NEG, 0.0)


def _moba_kernel(q_ref, k_ref, v_ref, o_ref, s_ref, p_ref, v_aug_ref, k_aug_ref, q_aug_ref, kmean_ref):
    seq = k_ref.shape[0]
    n_blocks = seq // MOBA_BLOCK
    _fill_v_aug(v_ref, v_aug_ref)

    kf = k_ref[...].astype(F32)
    k_lane = lax.broadcasted_iota(jnp.int32, (seq, LANES), 1)
    k_block = lax.broadcasted_iota(jnp.int32, (seq, LANES), 0) // MOBA_BLOCK
    low = k_lane < MOBA_HEAD_DIM
    k_aug_ref[0] = jnp.where(low, kf, (k_lane - MOBA_HEAD_DIM == k_block).astype(F32)).astype(BF16)
    k_aug_ref[1] = jnp.where(low, (k_lane == k_block).astype(F32), kf).astype(BF16)

    kmean_ref[...] = jnp.zeros(kmean_ref.shape, F32)
    for n in range(n_blocks - 1):
        mean = jnp.sum(kf[_blk(n)], axis=0, keepdims=True) * (1.0 / MOBA_BLOCK)
        kmean_ref[n:n + 1, :] = mean
        kmean_ref[MOBA_HEAD_DIM + n:MOBA_HEAD_DIM + n + 1, :] = mean
    kmean = kmean_ref[...]
    kmean_hi = kmean.astype(BF16)
    kmean_lo = (kmean - kmean_hi.astype(F32)).astype(BF16)

    lane = lax.broadcasted_iota(jnp.int32, (ATT_BLOCK, LANES), 1)

    def fill_queries(own):
        qf = q_ref[_blk(own), :].astype(F32)
        q_even = jnp.where(lane < MOBA_HEAD_DIM, qf, 0.0)
        q_odd = jnp.where(lane >= MOBA_HEAD_DIM, qf, 0.0)
        if own > MOBA_TOPK:
            def gate(qh):
                qb = qh.astype(BF16)
                return _dot_nt(qb, kmean_hi) + _dot_nt(qb, kmean_lo)
            q_even = q_even + _moba_select(gate(q_even), MOBA_HEAD_DIM, own)
            q_odd = q_odd + _moba_select(gate(q_odd), 0, own)
        q_aug_ref[0, _blk(own), :] = q_even.astype(BF16)
        q_aug_ref[1, _blk(own), :] = q_odd.astype(BF16)

    def scores(own, s_buf):
        def block(kb):
            return jnp.concatenate([_dot_nt(q_aug_ref[0, _blk(own), :], k_aug_ref[0, _blk(kb), :]),
                                    _dot_nt(q_aug_ref[1, _blk(own), :], k_aug_ref[1, _blk(kb), :])], axis=0)
        return _tile_scores(own, block, s_buf)

    def finish(own, out):
        o_ref[_blk(own), :] = jnp.where(lane < MOBA_HEAD_DIM, out[:ATT_BLOCK], out[ATT_BLOCK:]).astype(BF16)

    plain = [own for own in range(n_blocks) if own <= MOBA_TOPK]
    selecting = [own for own in range(n_blocks) if own > MOBA_TOPK]
    for own in plain:
        fill_queries(own)
    first = plain[::-1][:SCORE_AHEAD]
    order = first + selecting[::-1] + [own for own in plain[::-1] if own not in first]
    _pipelined_tiles(order, scores, s_ref, p_ref, v_aug_ref, finish,
                     after_prologue=lambda: [fill_queries(own) for own in selecting])


def _moba_call(mq, mk, mv, batch, seq):
    t = mq.shape[0]
    seq_block = pl.BlockSpec((seq, LANES), lambda b, h: (b, h))
    return pl.pallas_call(
        _moba_kernel,
        grid=(batch, MOBA_HEADS // 2),
        in_specs=[seq_block, seq_block, seq_block],
        out_specs=seq_block,
        out_shape=jax.ShapeDtypeStruct((t, BRANCH_W), BF16),
        scratch_shapes=_att_scratch(seq) + [pltpu.VMEM((2, seq, LANES), BF16),
                                            pltpu.VMEM((2, seq, LANES), BF16),
                                            pltpu.VMEM((LANES, LANES), F32)],
        compiler_params=_params(2),
        name="moba",
    )(mq, mk, mv)


def _merge_kernel(x_ref, fa_ref, fb_ref, fc_ref, wg0_ref, wg1_ref, wg2_ref, bg_ref, wa_ref, wb_ref, wc_ref,
                  wo_ref, g_ref, b_ref, o_ref):
    for s in range(x_ref.shape[0] // SUB_TILE):
        x = x_ref[_sub(s), :]
        xb = x.astype(BF16)
        m = None
        for j, (f_ref, wg_ref, w_ref) in enumerate(((fa_ref, wg0_ref, wa_ref), (fb_ref, wg1_ref, wb_ref),
                                                   (fc_ref, wg2_ref, wc_ref))):
            gate = _sigmoid(_dot(xb, wg_ref[...]) + bg_ref[:, j * D_MODEL:(j + 1) * D_MODEL])
            term = gate * _dot(f_ref[_sub(s), :], w_ref[...])
            m = term if m is None else m + term
        z = _dot(m.astype(BF16), wo_ref[...])
        o_ref[_sub(s), :] = _layer_norm(DEEPNORM_ALPHA * x + z, g_ref[...], b_ref[...])


def _merge_call(x, fa, fb, fc, w_in_b, b_gate, w_a, w_b, w_c, w_o, ln_g, ln_b, layer, tm=ROW_TILE):
    t = x.shape[0]
    row = lambda i: (i, 0)
    vec = lambda n: pl.BlockSpec((1, n), lambda i: (0, 0))
    feat = pl.BlockSpec((tm, BRANCH_W), row)
    gate_cols = [_layer_weight((D_MODEL, D_MODEL), layer, PROJ_COLS // D_MODEL + j) for j in range(N_BRANCHES)]
    branch_w = _layer_weight((BRANCH_W, D_MODEL), layer)
    return pl.pallas_call(
        _merge_kernel,
        grid=(t // tm,),
        in_specs=[pl.BlockSpec((tm, D_MODEL), row), feat, feat, feat, *gate_cols, vec(N_BRANCHES * D_MODEL),
                  branch_w, branch_w, branch_w, _layer_weight((D_MODEL, D_MODEL), layer),
                  vec(D_MODEL), vec(D_MODEL)],
        out_specs=pl.BlockSpec((tm, D_MODEL), row),
        out_shape=jax.ShapeDtypeStruct((t, D_MODEL), F32),
        compiler_params=_params(1),
        name="merge",
    )(x, fa, fb, fc, w_in_b, w_in_b, w_in_b, b_gate, w_a, w_b, w_c, w_o, ln_g, ln_b)


FF_CHUNK = 1024


def _ffn_kernel(x_ref, w1_ref, b1_ref, w2_ref, b2_ref, g_ref, b_ref, o_ref):
    for s in range(x_ref.shape[0] // SUB_TILE):
        x = x_ref[_sub(s), :]
        xb = x.astype(BF16)
        y = None
        for c in range(D_FF // FF_CHUNK):
            cols = slice(c * FF_CHUNK, (c + 1) * FF_CHUNK)
            h = jnp.maximum(_dot(xb, w1_ref[:, cols]) + b1_ref[:, cols], 0.0)
            part = _dot((h * h).astype(BF16), w2_ref[cols, :])
            y = part if y is None else y + part
        o_ref[_sub(s), :] = _layer_norm(DEEPNORM_ALPHA * x + y + b2_ref[...], g_ref[...], b_ref[...])


def _ffn_call(x, w1, b1, w2, b2, ln_g, ln_b, layer, tm=ROW_TILE):
    t = x.shape[0]
    row = lambda i: (i, 0)
    vec = lambda n: pl.BlockSpec((1, n), lambda i: (0, 0))
    return pl.pallas_call(
        _ffn_kernel,
        grid=(t // tm,),
        in_specs=[pl.BlockSpec((tm, D_MODEL), row),
                  _layer_weight((D_MODEL, D_FF), layer), vec(D_FF), _layer_weight((D_FF, D_MODEL), layer),
                  vec(D_MODEL), vec(D_MODEL), vec(D_MODEL)],
        out_specs=pl.BlockSpec((tm, D_MODEL), row),
        out_shape=jax.ShapeDtypeStruct((t, D_MODEL), F32),
        compiler_params=_params(1),
        name="ffn",
    )(x, w1, b1, w2, b2, ln_g, ln_b)


def _rope_tables(seq):
    pos = jnp.arange(seq, dtype=F32)
    inv = ROPE_THETA ** (-jnp.arange(0, DIFF_QK_DIM, 2, dtype=F32) / DIFF_QK_DIM)
    ang = pos[:, None] * inv[None, :]
    ang = jnp.concatenate([ang, ang], axis=-1)
    cos, sin = jnp.cos(ang), jnp.sin(ang)
    first_half = (jnp.arange(DIFF_QK_DIM) < HALF)[None, :]
    sin_a = jnp.where(first_half, -sin, 0.0)
    sin_b = jnp.where(first_half, 0.0, sin)
    tile2 = lambda a: jnp.concatenate([a, a], axis=-1)
    k_tabs = [tile2(cos), tile2(sin_a), tile2(sin_b)]
    scale = DIFF_QK_DIM ** -0.5 * math.log2(math.e)
    q_tabs = [a * scale for a in k_tabs]
    return q_tabs + k_tabs


def kernel(x, w_in, b_gate, conv_w, conv_b, conv_ln_g, conv_ln_b, w_conv_out, lam_q1, lam_k1, lam_q2, lam_k2,
           diff_subln_g, w_diff_out, w_moba_out, w_o, ln1_g, ln1_b, w_ff1, b_ff1, w_ff2, b_ff2, ln2_g, ln2_b):
    batch, seq, d = x.shape
    assert d == D_MODEL and seq % 512 == 0
    tables = _rope_tables(seq)
    row = lambda a: a.reshape(1, -1)
    xs = x.reshape(batch * seq, d)
    w_in_b, w_a, w_b, w_c, w_o_b, w1, w2 = (w.astype(BF16) for w in
                                            (w_in, w_conv_out, w_diff_out, w_moba_out, w_o, w_ff1, w_ff2))
    for l in range(DEPTH):
        h, dq, dk, dv, mq, mk, mv = _proj_call(xs, w_in_b, tables, seq, l)
        fa = _conv_call(h, conv_w[l, :, 0, :], row(conv_b[l]), row(conv_ln_g[l]), row(conv_ln_b[l]), batch, seq)
        fb = _diff_call(dq, dk, dv, row(lam_q1[l]), row(lam_k1[l]), row(lam_q2[l]), row(lam_k2[l]),
                        row(diff_subln_g[l]), l, batch, seq)
        fc = _moba_call(mq, mk, mv, batch, seq)
        xs = _merge_call(xs, fa, fb, fc, w_in_b, row(b_gate[l]), w_a, w_b, w_c, w_o_b,
                         row(ln1_g[l]), row(ln1_b[l]), l)
        xs = _ffn_call(xs, w1, row(b_ff1[l]), w2, row(b_ff2[l]), row(ln2_g[l]), row(ln2_b[l]), l)
    return xs.reshape(batch, seq, d)
```

```python
import functools
import math

import jax
import jax.numpy as jnp
from jax import lax
from jax.experimental import pallas as pl
from jax.experimental.pallas import tpu as pltpu

F32 = jnp.float32
BF16 = jnp.bfloat16

D_MODEL = 1024
DEPTH = 4
CONV_CH = D_MODEL // 2
CONV_WIDTH = 31
DIFF_QK_DIM = 64
DIFF_V_DIM = 2 * DIFF_QK_DIM
DIFF_HEADS = D_MODEL // 256
MOBA_HEAD_DIM = 64
MOBA_HEADS = D_MODEL // 128
MOBA_BLOCK = 256
MOBA_TOPK = 3
D_FF = 4 * D_MODEL
ROPE_THETA = 10000.0
N_BRANCHES = 3
LN_EPS = 1e-5
NEG = -1e30
DEEPNORM_ALPHA = (2.0 * DEPTH) ** 0.25

BRANCH_W = 512
PROJ_COLS = 8 * BRANCH_W
LANES = 128
SUBLANES = 8
HALF = DIFF_QK_DIM // 2
ATT_BLOCK = 256
SCORE_AHEAD = 3
SCORE_BUFFERS = SCORE_AHEAD + 1
ROW_TILE = 1024
SUB_TILE = 256
PROJ_SUB_TILE = 512
CONV_TILE = 1024
CONV_SUB_TILE = 256
CONV_HALO = 32
VMEM_LIMIT = 56 * 1024 * 1024


def _params(n_axes):
    return pltpu.CompilerParams(dimension_semantics=("arbitrary",) * n_axes,
                                vmem_limit_bytes=VMEM_LIMIT)


def _layer_weight(shape, layer, col_block=0):
    return pl.BlockSpec((None,) + shape, lambda i: (layer, 0, col_block), pipeline_mode=pl.Buffered(1))


def _sub(s):
    return slice(s * SUB_TILE, (s + 1) * SUB_TILE)


def _sigmoid(x):
    return 1.0 / (1.0 + jnp.exp(-x))


def _layer_norm(y, g, b):
    mu = jnp.mean(y, axis=-1, keepdims=True)
    d = y - mu
    var = jnp.mean(d * d, axis=-1, keepdims=True)
    return d * lax.rsqrt(var + LN_EPS) * g + b


def _dot(a, b):
    return jnp.dot(a, b, preferred_element_type=F32)


def _dot_nt(a, b):
    return lax.dot_general(a, b, (((1,), (1,)), ((), ())), preferred_element_type=F32)


def _rope(u, c, sa, sb):
    outs = []
    for j in range(u.shape[1] // LANES):
        uc = u[:, LANES * j:LANES * (j + 1)]
        fwd = pltpu.roll(uc, HALF, 1)
        bwd = pltpu.roll(uc, LANES - HALF, 1)
        outs.append(uc * c + bwd * sa + fwd * sb)
    return jnp.concatenate(outs, axis=1)


def _proj_kernel(x_ref, w_ref, cq_ref, sqa_ref, sqb_ref, ck_ref, ska_ref, skb_ref,
                 h_ref, dq_ref, dk_ref, dv_ref, mq_ref, mk_ref, mv_ref):
    for s in range(x_ref.shape[0] // PROJ_SUB_TILE):
        rows = slice(s * PROJ_SUB_TILE, (s + 1) * PROJ_SUB_TILE)
        xb = x_ref[rows, :].astype(BF16)

        def mm(group):
            return _dot(xb, w_ref[:, group * BRANCH_W:(group + 1) * BRANCH_W])

        h_ref[rows, :] = mm(0) * _sigmoid(mm(1))
        cq, sqa, sqb = cq_ref[rows, :], sqa_ref[rows, :], sqb_ref[rows, :]
        ck, ska, skb = ck_ref[rows, :], ska_ref[rows, :], skb_ref[rows, :]
        dq_ref[rows, :] = _rope(mm(2), cq, sqa, sqb).astype(BF16)
        dk_ref[rows, :] = _rope(mm(3), ck, ska, skb).astype(BF16)
        dv_ref[rows, :] = mm(4).astype(BF16)
        mq_ref[rows, :] = _rope(mm(5), cq, sqa, sqb).astype(BF16)
        mk_ref[rows, :] = _rope(mm(6), ck, ska, skb).astype(BF16)
        mv_ref[rows, :] = mm(7).astype(BF16)


def _proj_call(x, w_in_b, tables, seq, layer, tm=ROW_TILE):
    t = x.shape[0]
    n_seq_tiles = seq // tm
    row = lambda i: (i, 0)
    tab = lambda i: (i % n_seq_tiles, 0)
    out_spec = pl.BlockSpec((tm, BRANCH_W), row)
    return pl.pallas_call(
        _proj_kernel,
        grid=(t // tm,),
        in_specs=[pl.BlockSpec((tm, D_MODEL), row), _layer_weight((D_MODEL, PROJ_COLS), layer)]
                 + [pl.BlockSpec((tm, LANES), tab)] * 6,
        out_specs=[out_spec] * 7,
        out_shape=[jax.ShapeDtypeStruct((t, BRANCH_W), F32)]
                  + [jax.ShapeDtypeStruct((t, BRANCH_W), BF16)] * 6,
        compiler_params=_params(1),
        name="proj",
    )(x, w_in_b, *tables)


def _conv_kernel(prev_ref, cur_ref, w_ref, b_ref, g_ref, beta_ref, o_ref, pad_ref):
    i = pl.program_id(1)
    keep = (i > 0).astype(F32)
    pad_ref[0:CONV_HALO, :] = prev_ref[...] * keep
    pad_ref[CONV_HALO:CONV_HALO + CONV_TILE, :] = cur_ref[...]
    base = CONV_HALO - (CONV_WIDTH - 1)
    for s in range(CONV_TILE // CONV_SUB_TILE):
        first = s * CONV_SUB_TILE
        acc = jnp.zeros((CONV_SUB_TILE, CONV_CH), F32) + b_ref[...]
        for r in range(SUBLANES):
            taps = [k for k in range(CONV_WIDTH) if (base + k) % SUBLANES == r]
            last = base + taps[-1]
            slab = pad_ref[first + r:first + last + CONV_SUB_TILE, :]
            part = None
            for k in taps:
                off = base + k - r
                term = slab[off:off + CONV_SUB_TILE, :] * w_ref[k:k + 1, :]
                part = term if part is None else part + term
            acc = acc + part
        y = _layer_norm(acc, g_ref[...], beta_ref[...])
        o_ref[first:first + CONV_SUB_TILE, :] = (y * _sigmoid(y)).astype(BF16)


def _conv_call(h, conv_w, conv_b, ln_g, ln_b, batch, seq):
    t = h.shape[0]
    n_tiles = seq // CONV_TILE
    halo_per_tile = CONV_TILE // CONV_HALO
    cur = lambda b, i: (b * n_tiles + i, 0)
    prev = lambda b, i: (jnp.maximum((b * n_tiles + i) * halo_per_tile - 1, 0), 0)
    vec = pl.BlockSpec((1, CONV_CH), lambda b, i: (0, 0))
    return pl.pallas_call(
        _conv_kernel,
        grid=(batch, n_tiles),
        in_specs=[pl.BlockSpec((CONV_HALO, CONV_CH), prev),
                  pl.BlockSpec((CONV_TILE, CONV_CH), cur),
                  pl.BlockSpec((CONV_WIDTH, CONV_CH), lambda b, i: (0, 0)),
                  vec, vec, vec],
        out_specs=pl.BlockSpec((CONV_TILE, CONV_CH), cur),
        out_shape=jax.ShapeDtypeStruct((t, CONV_CH), BF16),
        scratch_shapes=[pltpu.VMEM((CONV_HALO + CONV_TILE, CONV_CH), F32)],
        compiler_params=_params(2),
        name="conv",
    )(h, h, conv_w, conv_b, ln_g, ln_b)


MAPS = 2
ROWS = MAPS * ATT_BLOCK


def _blk(kb):
    return slice(kb * ATT_BLOCK, (kb + 1) * ATT_BLOCK)


def _causal_mask():
    r = lax.broadcasted_iota(jnp.int32, (ROWS, ATT_BLOCK), 0) % ATT_BLOCK
    c = lax.broadcasted_iota(jnp.int32, (ROWS, ATT_BLOCK), 1)
    return c <= r


def _fill_v_aug(v_ref, v_aug_ref):
    v_aug_ref[:, :LANES] = v_ref[...]
    v_aug_ref[:, LANES:] = jnp.ones((v_ref.shape[0], LANES), BF16)


def _tile_scores(tile, score_fn, s_ref):
    m_part = None
    for kb in range(tile + 1):
        s = score_fn(kb)
        if kb == tile:
            s = jnp.where(_causal_mask(), s, NEG)
        s_ref[:, _blk(kb)] = s
        folded = jnp.maximum(s[:, :LANES], s[:, LANES:])
        m_part = folded if m_part is None else jnp.maximum(m_part, folded)
    return jnp.max(m_part, axis=-1, keepdims=True)


def _pipelined_tiles(order, scores, s_ref, p_ref, v_aug_ref, finish, after_prologue=None):
    buf = {tile: pos % SCORE_BUFFERS for pos, tile in enumerate(order)}
    maxima = {tile: scores(tile, s_ref.at[buf[tile]]) for tile in order[:SCORE_AHEAD]}
    if after_prologue is not None:
        after_prologue()
    for pos, tile in enumerate(order):
        m, b = maxima.pop(tile), buf[tile]
        for kb in range(tile + 1):
            p_ref[b, :, _blk(kb)] = jnp.exp2(s_ref[b, :, _blk(kb)] - m).astype(BF16)
        if pos + SCORE_AHEAD < len(order):
            ahead = order[pos + SCORE_AHEAD]
            maxima[ahead] = scores(ahead, s_ref.at[buf[ahead]])
        n_keys = (tile + 1) * ATT_BLOCK
        acc = _dot(p_ref[b, :, :n_keys], v_aug_ref[:n_keys, :])
        finish(tile, acc[:, :LANES] / acc[:, LANES:])


def _att_scratch(seq):
    return [pltpu.VMEM((SCORE_BUFFERS, ROWS, seq), F32),
            pltpu.VMEM((SCORE_BUFFERS, ROWS, seq), BF16),
            pltpu.VMEM((seq, 2 * LANES), BF16)]


def _diff_kernel(q_ref, k_ref, v_ref, lq1_ref, lk1_ref, lq2_ref, lk2_ref, g_ref, o_ref,
                 s_ref, p_ref, v_aug_ref, *, lambda_init):
    lam = (jnp.exp(jnp.sum(lq1_ref[...] * lk1_ref[...], axis=-1, keepdims=True))
           - jnp.exp(jnp.sum(lq2_ref[...] * lk2_ref[...], axis=-1, keepdims=True)) + lambda_init)
    _fill_v_aug(v_ref, v_aug_ref)
    lane = lax.broadcasted_iota(jnp.int32, (ATT_BLOCK, LANES), 1)

    def scores(tile, s_buf):
        qf = q_ref[_blk(tile), :].astype(F32)
        qq = jnp.concatenate([jnp.where(lane < DIFF_QK_DIM, qf, 0.0),
                              jnp.where(lane >= DIFF_QK_DIM, qf, 0.0)], axis=0).astype(BF16)
        return _tile_scores(tile, lambda kb: _dot_nt(qq, k_ref[_blk(kb), :]), s_buf)

    def finish(tile, out):
        o = out[:ATT_BLOCK] - lam * out[ATT_BLOCK:]
        o = o * lax.rsqrt(jnp.mean(o * o, axis=-1, keepdims=True) + LN_EPS) * g_ref[...]
        o_ref[_blk(tile), :] = (o * (1.0 - lambda_init)).astype(BF16)

    _pipelined_tiles(list(reversed(range(k_ref.shape[0] // ATT_BLOCK))), scores, s_ref, p_ref, v_aug_ref, finish)


def _diff_call(dq, dk, dv, lq1, lk1, lq2, lk2, subln_g, layer_idx, batch, seq):
    t = dq.shape[0]
    lambda_init = 0.8 - 0.6 * math.exp(-0.3 * layer_idx)
    lam_spec = pl.BlockSpec((1, DIFF_QK_DIM), lambda b, h: (0, 0))
    seq_block = pl.BlockSpec((seq, LANES), lambda b, h: (b, h))
    return pl.pallas_call(
        functools.partial(_diff_kernel, lambda_init=lambda_init),
        grid=(batch, DIFF_HEADS),
        in_specs=[seq_block, seq_block, seq_block,
                  lam_spec, lam_spec, lam_spec, lam_spec,
                  pl.BlockSpec((1, DIFF_V_DIM), lambda b, h: (0, 0))],
        out_specs=seq_block,
        out_shape=jax.ShapeDtypeStruct((t, BRANCH_W), BF16),
        scratch_shapes=_att_scratch(seq),
        compiler_params=_params(2),
        name="diff_attn",
    )(dq, dk, dv, lq1, lk1, lq2, lk2, subln_g)


def _moba_select(gate, first, own):
    lane = lax.broadcasted_iota(jnp.int32, gate.shape, 1)
    lane_f = lane.astype(F32)
    neg_inf = -jnp.inf
    cand = jnp.logical_and(lane >= first, lane < first + own)
    gv = jnp.where(cand, gate, neg_inf)
    for _ in range(MOBA_TOPK):
        mx = jnp.max(gv, axis=-1, keepdims=True)
        is_mx = jnp.logical_and(gv == mx, gv > neg_inf)
        idx = jnp.min(jnp.where(is_mx, lane_f, float(LANES)), axis=-1, keepdims=True)
        gv = jnp.where(lane_f == idx, neg_inf, gv)
    return jnp.where(gv > neg_inf, NEG, 0.0)


def _moba_kernel(q_ref, k_ref, v_ref, o_ref, s_ref, p_ref, v_aug_ref, k_aug_ref, q_aug_ref, kmean_ref):
    seq = k_ref.shape[0]
    n_blocks = seq // MOBA_BLOCK
    _fill_v_aug(v_ref, v_aug_ref)

    kf = k_ref[...].astype(F32)
    k_lane = lax.broadcasted_iota(jnp.int32, (seq, LANES), 1)
    k_block = lax.broadcasted_iota(jnp.int32, (seq, LANES), 0) // MOBA_BLOCK
    low = k_lane < MOBA_HEAD_DIM
    k_aug_ref[:, :LANES] = jnp.where(low, kf, (k_lane - MOBA_HEAD_DIM == k_block).astype(F32)).astype(BF16)
    k_aug_ref[:, LANES:] = jnp.where(low, (k_lane == k_block).astype(F32), kf).astype(BF16)

    kmean_ref[...] = jnp.zeros(kmean_ref.shape, F32)
    mean_lane = lax.broadcasted_iota(jnp.int32, (1, LANES), 1)
    for n in range(n_blocks - 1):
        mean = jnp.sum(kf[_blk(n)], axis=0, keepdims=True) * (1.0 / MOBA_BLOCK)
        kmean_ref[n:n + 1, :] = jnp.where(mean_lane >= MOBA_HEAD_DIM, mean, 0.0)
        kmean_ref[MOBA_HEAD_DIM + n:MOBA_HEAD_DIM + n + 1, :] = jnp.where(mean_lane < MOBA_HEAD_DIM, mean, 0.0)
    kmean = kmean_ref[...]
    kmean_hi = kmean.astype(BF16)
    kmean_lo = (kmean - kmean_hi.astype(F32)).astype(BF16)
    kmean_pieces = jnp.concatenate([kmean_hi, kmean_lo], axis=1)

    lane = lax.broadcasted_iota(jnp.int32, (ATT_BLOCK, LANES), 1)

    def fill_queries(own):
        q = q_ref[_blk(own), :]
        qf = q.astype(F32)
        q_even = jnp.where(lane < MOBA_HEAD_DIM, qf, 0.0)
        q_odd = jnp.where(lane >= MOBA_HEAD_DIM, qf, 0.0)
        if own > MOBA_TOPK:
            gate = _dot_nt(jnp.concatenate([q, q], axis=1), kmean_pieces)
            q_even = q_even + _moba_select(gate, MOBA_HEAD_DIM, own)
            q_odd = q_odd + _moba_select(gate, 0, own)
        zeros = jnp.zeros((ATT_BLOCK, LANES), BF16)
        q_aug_ref[own, :ATT_BLOCK, :LANES] = q_even.astype(BF16)
        q_aug_ref[own, :ATT_BLOCK, LANES:] = zeros
        q_aug_ref[own, ATT_BLOCK:, :LANES] = zeros
        q_aug_ref[own, ATT_BLOCK:, LANES:] = q_odd.astype(BF16)

    def scores(own, s_buf):
        return _tile_scores(own, lambda kb: _dot_nt(q_aug_ref[own], k_aug_ref[_blk(kb), :]), s_buf)

    def finish(own, out):
        o_ref[_blk(own), :] = jnp.where(lane < MOBA_HEAD_DIM, out[:ATT_BLOCK], out[ATT_BLOCK:]).astype(BF16)

    plain = [own for own in range(n_blocks) if own <= MOBA_TOPK]
    selecting = [own for own in range(n_blocks) if own > MOBA_TOPK]
    for own in plain:
        fill_queries(own)
    first = plain[::-1][:SCORE_AHEAD]
    order = first + selecting[::-1] + [own for own in plain[::-1] if own not in first]
    _pipelined_tiles(order, scores, s_ref, p_ref, v_aug_ref, finish,
                     after_prologue=lambda: [fill_queries(own) for own in selecting])


def _moba_call(mq, mk, mv, batch, seq):
    t = mq.shape[0]
    seq_block = pl.BlockSpec((seq, LANES), lambda b, h: (b, h))
    return pl.pallas_call(
        _moba_kernel,
        grid=(batch, MOBA_HEADS // 2),
        in_specs=[seq_block, seq_block, seq_block],
        out_specs=seq_block,
        out_shape=jax.ShapeDtypeStruct((t, BRANCH_W), BF16),
        scratch_shapes=_att_scratch(seq) + [pltpu.VMEM((seq, 2 * LANES), BF16),
                                            pltpu.VMEM((seq // ATT_BLOCK, ROWS, 2 * LANES), BF16),
                                            pltpu.VMEM((LANES, LANES), F32)],
        compiler_params=_params(2),
        name="moba",
    )(mq, mk, mv)


def _merge_kernel(x_ref, fa_ref, fb_ref, fc_ref, wg0_ref, wg1_ref, wg2_ref, bg_ref, wa_ref, wb_ref, wc_ref,
                  wo_ref, g_ref, b_ref, o_ref):
    for s in range(x_ref.shape[0] // SUB_TILE):
        x = x_ref[_sub(s), :]
        xb = x.astype(BF16)
        m = None
        for j, (f_ref, wg_ref, w_ref) in enumerate(((fa_ref, wg0_ref, wa_ref), (fb_ref, wg1_ref, wb_ref),
                                                   (fc_ref, wg2_ref, wc_ref))):
            gate = _sigmoid(_dot(xb, wg_ref[...]) + bg_ref[:, j * D_MODEL:(j + 1) * D_MODEL])
            term = gate * _dot(f_ref[_sub(s), :], w_ref[...])
            m = term if m is None else m + term
        z = _dot(m.astype(BF16), wo_ref[...])
        o_ref[_sub(s), :] = _layer_norm(DEEPNORM_ALPHA * x + z, g_ref[...], b_ref[...])


def _merge_call(x, fa, fb, fc, w_in_b, b_gate, w_a, w_b, w_c, w_o, ln_g, ln_b, layer, tm=ROW_TILE):
    t = x.shape[0]
    row = lambda i: (i, 0)
    vec = lambda n: pl.BlockSpec((1, n), lambda i: (0, 0))
    feat = pl.BlockSpec((tm, BRANCH_W), row)
    gate_cols = [_layer_weight((D_MODEL, D_MODEL), layer, PROJ_COLS // D_MODEL + j) for j in range(N_BRANCHES)]
    branch_w = _layer_weight((BRANCH_W, D_MODEL), layer)
    return pl.pallas_call(
        _merge_kernel,
        grid=(t // tm,),
        in_specs=[pl.BlockSpec((tm, D_MODEL), row), feat, feat, feat, *gate_cols, vec(N_BRANCHES * D_MODEL),
                  branch_w, branch_w, branch_w, _layer_weight((D_MODEL, D_MODEL), layer),
                  vec(D_MODEL), vec(D_MODEL)],
        out_specs=pl.BlockSpec((tm, D_MODEL), row),
        out_shape=jax.ShapeDtypeStruct((t, D_MODEL), F32),
        compiler_params=_params(1),
        name="merge",
    )(x, fa, fb, fc, w_in_b, w_in_b, w_in_b, b_gate, w_a, w_b, w_c, w_o, ln_g, ln_b)


FF_CHUNK = 1024


def _ffn_kernel(x_ref, w1_ref, b1_ref, w2_ref, b2_ref, g_ref, b_ref, o_ref):
    for s in range(x_ref.shape[0] // SUB_TILE):
        x = x_ref[_sub(s), :]
        xb = x.astype(BF16)
        y = None
        for c in range(D_FF // FF_CHUNK):
            cols = slice(c * FF_CHUNK, (c + 1) * FF_CHUNK)
            h = jnp.maximum(_dot(xb, w1_ref[:, cols]) + b1_ref[:, cols], 0.0)
            part = _dot((h * h).astype(BF16), w2_ref[cols, :])
            y = part if y is None else y + part
        o_ref[_sub(s), :] = _layer_norm(DEEPNORM_ALPHA * x + y + b2_ref[...], g_ref[...], b_ref[...])


def _ffn_call(x, w1, b1, w2, b2, ln_g, ln_b, layer, tm=ROW_TILE):
    t = x.shape[0]
    row = lambda i: (i, 0)
    vec = lambda n: pl.BlockSpec((1, n), lambda i: (0, 0))
    return pl.pallas_call(
        _ffn_kernel,
        grid=(t // tm,),
        in_specs=[pl.BlockSpec((tm, D_MODEL), row),
                  _layer_weight((D_MODEL, D_FF), layer), vec(D_FF), _layer_weight((D_FF, D_MODEL), layer),
                  vec(D_MODEL), vec(D_MODEL), vec(D_MODEL)],
        out_specs=pl.BlockSpec((tm, D_MODEL), row),
        out_shape=jax.ShapeDtypeStruct((t, D_MODEL), F32),
        compiler_params=_params(1),
        name="ffn",
    )(x, w1, b1, w2, b2, ln_g, ln_b)


def _rope_tables(seq):
    pos = jnp.arange(seq, dtype=F32)
    inv = ROPE_THETA ** (-jnp.arange(0, DIFF_QK_DIM, 2, dtype=F32) / DIFF_QK_DIM)
    ang = pos[:, None] * inv[None, :]
    ang = jnp.concatenate([ang, ang], axis=-1)
    cos, sin = jnp.cos(ang), jnp.sin(ang)
    first_half = (jnp.arange(DIFF_QK_DIM) < HALF)[None, :]
    sin_a = jnp.where(first_half, -sin, 0.0)
    sin_b = jnp.where(first_half, 0.0, sin)
    tile2 = lambda a: jnp.concatenate([a, a], axis=-1)
    k_tabs = [tile2(cos), tile2(sin_a), tile2(sin_b)]
    scale = DIFF_QK_DIM ** -0.5 * math.log2(math.e)
    q_tabs = [a * scale for a in k_tabs]
    return q_tabs + k_tabs


def kernel(x, w_in, b_gate, conv_w, conv_b, conv_ln_g, conv_ln_b, w_conv_out, lam_q1, lam_k1, lam_q2, lam_k2,
           diff_subln_g, w_diff_out, w_moba_out, w_o, ln1_g, ln1_b, w_ff1, b_ff1, w_ff2, b_ff2, ln2_g, ln2_b):
    batch, seq, d = x.shape
    assert d == D_MODEL and seq % 512 == 0
    tables = _rope_tables(seq)
    row = lambda a: a.reshape(1, -1)
    xs = x.reshape(batch * seq, d)
    w_in_b, w_a, w_b, w_c, w_o_b, w1, w2 = (w.astype(BF16) for w in
                                            (w_in, w_conv_out, w_diff_out, w_moba_out, w_o, w_ff1, w_ff2))
    for l in range(DEPTH):
        h, dq, dk, dv, mq, mk, mv = _proj_call(xs, w_in_b, tables, seq, l)
        fa = _conv_call(h, conv_w[l, :, 0, :], row(conv_b[l]), row(conv_ln_g[l]), row(conv_ln_b[l]), batch, seq)
        fb = _diff_call(dq, dk, dv, row(lam_q1[l]), row(lam_k1[l]), row(lam_q2[l]), row(lam_k2[l]),
                        row(diff_subln_g[l]), l, batch, seq)
        fc = _moba_call(mq, mk, mv, batch, seq)
        xs = _merge_call(xs, fa, fb, fc, w_in_b, row(b_gate[l]), w_a, w_b, w_c, w_o_b,
                         row(ln1_g[l]), row(ln1_b[l]), l)
        xs = _ffn_call(xs, w1, row(b_ff1[l]), w2, row(b_ff2[l]), row(ln2_g[l]), row(ln2_b[l]), l)
    return xs.reshape(batch, seq, d)
```

```python
import functools
import math

import jax
import jax.numpy as jnp
from jax import lax
from jax.experimental import pallas as pl
from jax.experimental.pallas import tpu as pltpu

F32 = jnp.float32
BF16 = jnp.bfloat16

D_MODEL = 1024
DEPTH = 4
CONV_CH = D_MODEL // 2
CONV_WIDTH = 31
DIFF_QK_DIM = 64
DIFF_V_DIM = 2 * DIFF_QK_DIM
DIFF_HEADS = D_MODEL // 256
MOBA_HEAD_DIM = 64
MOBA_HEADS = D_MODEL // 128
MOBA_BLOCK = 256
MOBA_TOPK = 3
D_FF = 4 * D_MODEL
ROPE_THETA = 10000.0
N_BRANCHES = 3
LN_EPS = 1e-5
NEG = -1e30
DEEPNORM_ALPHA = (2.0 * DEPTH) ** 0.25

BRANCH_W = 512
PROJ_COLS = 8 * BRANCH_W
LANES = 128
SUBLANES = 8
HALF = DIFF_QK_DIM // 2
ATT_BLOCK = 256
SCORE_AHEAD = 3
SCORE_BUFFERS = SCORE_AHEAD + 1
ROW_TILE = 1024
SUB_TILE = 256
PROJ_SUB_TILE = 512
CONV_TILE = 1024
CONV_SUB_TILE = 256
CONV_HALO = 32
VMEM_LIMIT = 56 * 1024 * 1024


def _params(n_axes):
    return pltpu.CompilerParams(dimension_semantics=("arbitrary",) * n_axes,
                                vmem_limit_bytes=VMEM_LIMIT)


def _layer_weight(shape, layer, col_block=0):
    return pl.BlockSpec((None,) + shape, lambda i: (layer, 0, col_block), pipeline_mode=pl.Buffered(1))


def _sub(s):
    return slice(s * SUB_TILE, (s + 1) * SUB_TILE)


def _sigmoid(x):
    return 1.0 / (1.0 + jnp.exp(-x))


def _layer_norm(y, g, b):
    mu = jnp.mean(y, axis=-1, keepdims=True)
    d = y - mu
    var = jnp.mean(d * d, axis=-1, keepdims=True)
    return d * lax.rsqrt(var + LN_EPS) * g + b


def _dot(a, b):
    return jnp.dot(a, b, preferred_element_type=F32)


def _dot_nt(a, b):
    return lax.dot_general(a, b, (((1,), (1,)), ((), ())), preferred_element_type=F32)


def _rope(u, c, sa, sb):
    outs = []
    for j in range(u.shape[1] // LANES):
        uc = u[:, LANES * j:LANES * (j + 1)]
        fwd = pltpu.roll(uc, HALF, 1)
        bwd = pltpu.roll(uc, LANES - HALF, 1)
        outs.append(uc * c + bwd * sa + fwd * sb)
    return jnp.concatenate(outs, axis=1)


def _proj_kernel(x_ref, w_ref, cq_ref, sqa_ref, sqb_ref, ck_ref, ska_ref, skb_ref,
                 h_ref, dq_ref, dk_ref, dv_ref, mq_ref, mk_ref, mv_ref):
    for s in range(x_ref.shape[0] // PROJ_SUB_TILE):
        rows = slice(s * PROJ_SUB_TILE, (s + 1) * PROJ_SUB_TILE)
        xb = x_ref[rows, :].astype(BF16)

        def mm(group):
            return _dot(xb, w_ref[:, group * BRANCH_W:(group + 1) * BRANCH_W])

        h_ref[rows, :] = mm(0) * _sigmoid(mm(1))
        cq, sqa, sqb = cq_ref[rows, :], sqa_ref[rows, :], sqb_ref[rows, :]
        ck, ska, skb = ck_ref[rows, :], ska_ref[rows, :], skb_ref[rows, :]
        dq_ref[rows, :] = _rope(mm(2), cq, sqa, sqb).astype(BF16)
        dk_ref[rows, :] = _rope(mm(3), ck, ska, skb).astype(BF16)
        dv_ref[rows, :] = mm(4).astype(BF16)
        mq_ref[rows, :] = _rope(mm(5), cq, sqa, sqb).astype(BF16)
        mk_ref[rows, :] = _rope(mm(6), ck, ska, skb).astype(BF16)
        mv_ref[rows, :] = mm(7).astype(BF16)


def _proj_call(x, w_in_b, tables, seq, layer, tm=ROW_TILE):
    t = x.shape[0]
    n_seq_tiles = seq // tm
    row = lambda i: (i, 0)
    tab = lambda i: (i % n_seq_tiles, 0)
    out_spec = pl.BlockSpec((tm, BRANCH_W), row)
    return pl.pallas_call(
        _proj_kernel,
        grid=(t // tm,),
        in_specs=[pl.BlockSpec((tm, D_MODEL), row), _layer_weight((D_MODEL, PROJ_COLS), layer)]
                 + [pl.BlockSpec((tm, LANES), tab)] * 6,
        out_specs=[out_spec] * 7,
        out_shape=[jax.ShapeDtypeStruct((t, BRANCH_W), F32)]
                  + [jax.ShapeDtypeStruct((t, BRANCH_W), BF16)] * 6,
        compiler_params=_params(1),
        name="proj",
    )(x, w_in_b, *tables)


def _conv_kernel(prev_ref, cur_ref, w_ref, b_ref, g_ref, beta_ref, o_ref, pad_ref):
    pad_ref[0:CONV_HALO, :] = jnp.where(pl.program_id(1) > 0, prev_ref[...], 0.0)
    pad_ref[CONV_HALO:CONV_HALO + CONV_TILE, :] = cur_ref[...]
    base = CONV_HALO - (CONV_WIDTH - 1)
    for s in range(CONV_TILE // CONV_SUB_TILE):
        first = s * CONV_SUB_TILE
        acc = jnp.zeros((CONV_SUB_TILE, CONV_CH), F32) + b_ref[...]
        for r in range(SUBLANES):
            taps = [k for k in range(CONV_WIDTH) if (base + k) % SUBLANES == r]
            last = base + taps[-1]
            slab = pad_ref[first + r:first + last + CONV_SUB_TILE, :]
            part = None
            for k in taps:
                off = base + k - r
                term = slab[off:off + CONV_SUB_TILE, :] * w_ref[k:k + 1, :]
                part = term if part is None else part + term
            acc = acc + part
        y = _layer_norm(acc, g_ref[...], beta_ref[...])
        o_ref[first:first + CONV_SUB_TILE, :] = (y * _sigmoid(y)).astype(BF16)


def _conv_call(h, conv_w, conv_b, ln_g, ln_b, batch, seq):
    t = h.shape[0]
    n_tiles = seq // CONV_TILE
    halo_per_tile = CONV_TILE // CONV_HALO
    cur = lambda b, i: (b * n_tiles + i, 0)
    prev = lambda b, i: (jnp.maximum((b * n_tiles + i) * halo_per_tile - 1, 0), 0)
    vec = pl.BlockSpec((1, CONV_CH), lambda b, i: (0, 0))
    return pl.pallas_call(
        _conv_kernel,
        grid=(batch, n_tiles),
        in_specs=[pl.BlockSpec((CONV_HALO, CONV_CH), prev),
                  pl.BlockSpec((CONV_TILE, CONV_CH), cur),
                  pl.BlockSpec((CONV_WIDTH, CONV_CH), lambda b, i: (0, 0)),
                  vec, vec, vec],
        out_specs=pl.BlockSpec((CONV_TILE, CONV_CH), cur),
        out_shape=jax.ShapeDtypeStruct((t, CONV_CH), BF16),
        scratch_shapes=[pltpu.VMEM((CONV_HALO + CONV_TILE, CONV_CH), F32)],
        compiler_params=_params(2),
        name="conv",
    )(h, h, conv_w, conv_b, ln_g, ln_b)


MAPS = 2
ROWS = MAPS * ATT_BLOCK


def _blk(kb):
    return slice(kb * ATT_BLOCK, (kb + 1) * ATT_BLOCK)


def _causal_mask():
    r = lax.broadcasted_iota(jnp.int32, (ROWS, ATT_BLOCK), 0) % ATT_BLOCK
    c = lax.broadcasted_iota(jnp.int32, (ROWS, ATT_BLOCK), 1)
    return c <= r


def _fill_v_aug(v_ref, v_aug_ref):
    v_aug_ref[:, :LANES] = v_ref[...]
    v_aug_ref[:, LANES:] = jnp.ones((v_ref.shape[0], LANES), BF16)


def _tile_scores(tile, score_fn, s_ref):
    m_part = None
    for kb in range(tile + 1):
        s = score_fn(kb)
        if kb == tile:
            s = jnp.where(_causal_mask(), s, NEG)
        s_ref[:, _blk(kb)] = s
        folded = jnp.maximum(s[:, :LANES], s[:, LANES:])
        m_part = folded if m_part is None else jnp.maximum(m_part, folded)
    return jnp.max(m_part, axis=-1, keepdims=True)


def _pipelined_tiles(order, scores, s_ref, p_ref, v_aug_ref, finish, after_prologue=None):
    buf = {tile: pos % SCORE_BUFFERS for pos, tile in enumerate(order)}
    maxima = {tile: scores(tile, s_ref.at[buf[tile]]) for tile in order[:SCORE_AHEAD]}
    if after_prologue is not None:
        after_prologue()
    for pos, tile in enumerate(order):
        m, b = maxima.pop(tile), buf[tile]
        for kb in range(tile + 1):
            p_ref[b, :, _blk(kb)] = jnp.exp2(s_ref[b, :, _blk(kb)] - m).astype(BF16)
        if pos + SCORE_AHEAD < len(order):
            ahead = order[pos + SCORE_AHEAD]
            maxima[ahead] = scores(ahead, s_ref.at[buf[ahead]])
        n_keys = (tile + 1) * ATT_BLOCK
        acc = _dot(p_ref[b, :, :n_keys], v_aug_ref[:n_keys, :])
        finish(tile, acc[:, :LANES] / acc[:, LANES:])


def _att_scratch(seq):
    return [pltpu.VMEM((SCORE_BUFFERS, ROWS, seq), F32),
            pltpu.VMEM((SCORE_BUFFERS, ROWS, seq), BF16),
            pltpu.VMEM((seq, 2 * LANES), BF16)]


def _diff_kernel(q_ref, k_ref, v_ref, lq1_ref, lk1_ref, lq2_ref, lk2_ref, g_ref, o_ref,
                 s_ref, p_ref, v_aug_ref, *, lambda_init):
    lam = (jnp.exp(jnp.sum(lq1_ref[...] * lk1_ref[...], axis=-1, keepdims=True))
           - jnp.exp(jnp.sum(lq2_ref[...] * lk2_ref[...], axis=-1, keepdims=True)) + lambda_init)
    _fill_v_aug(v_ref, v_aug_ref)
    lane = lax.broadcasted_iota(jnp.int32, (ATT_BLOCK, LANES), 1)

    def scores(tile, s_buf):
        qf = q_ref[_blk(tile), :].astype(F32)
        qq = jnp.concatenate([jnp.where(lane < DIFF_QK_DIM, qf, 0.0),
                              jnp.where(lane >= DIFF_QK_DIM, qf, 0.0)], axis=0).astype(BF16)
        return _tile_scores(tile, lambda kb: _dot_nt(qq, k_ref[_blk(kb), :]), s_buf)

    def finish(tile, out):
        o = out[:ATT_BLOCK] - lam * out[ATT_BLOCK:]
        o = o * lax.rsqrt(jnp.mean(o * o, axis=-1, keepdims=True) + LN_EPS) * g_ref[...]
        o_ref[_blk(tile), :] = (o * (1.0 - lambda_init)).astype(BF16)

    _pipelined_tiles(list(reversed(range(k_ref.shape[0] // ATT_BLOCK))), scores, s_ref, p_ref, v_aug_ref, finish)


def _diff_call(dq, dk, dv, lq1, lk1, lq2, lk2, subln_g, layer_idx, batch, seq):
    t = dq.shape[0]
    lambda_init = 0.8 - 0.6 * math.exp(-0.3 * layer_idx)
    lam_spec = pl.BlockSpec((1, DIFF_QK_DIM), lambda b, h: (0, 0))
    seq_block = pl.BlockSpec((seq, LANES), lambda b, h: (b, h))
    return pl.pallas_call(
        functools.partial(_diff_kernel, lambda_init=lambda_init),
        grid=(batch, DIFF_HEADS),
        in_specs=[seq_block, seq_block, seq_block,
                  lam_spec, lam_spec, lam_spec, lam_spec,
                  pl.BlockSpec((1, DIFF_V_DIM), lambda b, h: (0, 0))],
        out_specs=seq_block,
        out_shape=jax.ShapeDtypeStruct((t, BRANCH_W), BF16),
        scratch_shapes=_att_scratch(seq),
        compiler_params=_params(2),
        name="diff_attn",
    )(dq, dk, dv, lq1, lk1, lq2, lk2, subln_g)


def _moba_select(gate, first, own):
    lane = lax.broadcasted_iota(jnp.int32, gate.shape, 1)
    lane_f = lane.astype(F32)
    neg_inf = -jnp.inf
    cand = jnp.logical_and(lane >= first, lane < first + own)
    gv = jnp.where(cand, gate, neg_inf)
    for _ in range(MOBA_TOPK):
        mx = jnp.max(gv, axis=-1, keepdims=True)
        is_mx = jnp.logical_and(gv == mx, gv > neg_inf)
        idx = jnp.min(jnp.where(is_mx, lane_f, float(LANES)), axis=-1, keepdims=True)
        gv = jnp.where(lane_f == idx, neg_inf, gv)
    return jnp.where(gv > neg_inf, NEG, 0.0)


def _moba_kernel(q_ref, k_ref, v_ref, o_ref, s_ref, p_ref, v_aug_ref, k_aug_ref, q_aug_ref, kmean_ref):
    seq = k_ref.shape[0]
    n_blocks = seq // MOBA_BLOCK
    _fill_v_aug(v_ref, v_aug_ref)

    kf = k_ref[...].astype(F32)
    k_lane = lax.broadcasted_iota(jnp.int32, (seq, LANES), 1)
    k_block = lax.broadcasted_iota(jnp.int32, (seq, LANES), 0) // MOBA_BLOCK
    low = k_lane < MOBA_HEAD_DIM
    k_aug_ref[0] = jnp.where(low, kf, (k_lane - MOBA_HEAD_DIM == k_block).astype(F32)).astype(BF16)
    k_aug_ref[1] = jnp.where(low, (k_lane == k_block).astype(F32), kf).astype(BF16)

    kmean_ref[...] = jnp.zeros(kmean_ref.shape, F32)
    for n in range(n_blocks - 1):
        mean = jnp.sum(kf[_blk(n)], axis=0, keepdims=True) * (1.0 / MOBA_BLOCK)
        kmean_ref[n:n + 1, :] = mean
        kmean_ref[MOBA_HEAD_DIM + n:MOBA_HEAD_DIM + n + 1, :] = mean
    kmean = kmean_ref[...]
    kmean_hi = kmean.astype(BF16)
    kmean_lo = (kmean - kmean_hi.astype(F32)).astype(BF16)

    lane = lax.broadcasted_iota(jnp.int32, (ATT_BLOCK, LANES), 1)

    def fill_queries(own):
        qf = q_ref[_blk(own), :].astype(F32)
        q_even = jnp.where(lane < MOBA_HEAD_DIM, qf, 0.0)
        q_odd = jnp.where(lane >= MOBA_HEAD_DIM, qf, 0.0)
        if own > MOBA_TOPK:
            def gate(qh):
                qb = qh.astype(BF16)
                return _dot_nt(qb, kmean_hi) + _dot_nt(qb, kmean_lo)
            q_even = q_even + _moba_select(gate(q_even), MOBA_HEAD_DIM, own)
            q_odd = q_odd + _moba_select(gate(q_odd), 0, own)
        q_aug_ref[0, _blk(own), :] = q_even.astype(BF16)
        q_aug_ref[1, _blk(own), :] = q_odd.astype(BF16)

    def scores(own, s_buf):
        def block(kb):
            return jnp.concatenate([_dot_nt(q_aug_ref[0, _blk(own), :], k_aug_ref[0, _blk(kb), :]),
                                    _dot_nt(q_aug_ref[1, _blk(own), :], k_aug_ref[1, _blk(kb), :])], axis=0)
        return _tile_scores(own, block, s_buf)

    def finish(own, out):
        o_ref[_blk(own), :] = jnp.where(lane < MOBA_HEAD_DIM, out[:ATT_BLOCK], out[ATT_BLOCK:]).astype(BF16)

    plain = [own for own in range(n_blocks) if own <= MOBA_TOPK]
    selecting = [own for own in range(n_blocks) if own > MOBA_TOPK]
    for own in plain:
        fill_queries(own)
    first = plain[::-1][:SCORE_AHEAD]
    order = first + selecting[::-1] + [own for own in plain[::-1] if own not in first]
    _pipelined_tiles(order, scores, s_ref, p_ref, v_aug_ref, finish,
                     after_prologue=lambda: [fill_queries(own) for own in selecting])


def _moba_call(mq, mk, mv, batch, seq):
    t = mq.shape[0]
    seq_block = pl.BlockSpec((seq, LANES), lambda b, h: (b, h))
    return pl.pallas_call(
        _moba_kernel,
        grid=(batch, MOBA_HEADS // 2),
        in_specs=[seq_block, seq_block, seq_block],
        out_specs=seq_block,
        out_shape=jax.ShapeDtypeStruct((t, BRANCH_W), BF16),
        scratch_shapes=_att_scratch(seq) + [pltpu.VMEM((2, seq, LANES), BF16),
                                            pltpu.VMEM((2, seq, LANES), BF16),
                                            pltpu.VMEM((LANES, LANES), F32)],
        compiler_params=_params(2),
        name="moba",
    )(mq, mk, mv)


def _merge_kernel(x_ref, fa_ref, fb_ref, fc_ref, wg0_ref, wg1_ref, wg2_ref, bg_ref, wa_ref, wb_ref, wc_ref,
                  wo_ref, g_ref, b_ref, o_ref):
    for s in range(x_ref.shape[0] // SUB_TILE):
        x = x_ref[_sub(s), :]
        xb = x.astype(BF16)
        m = None
        for j, (f_ref, wg_ref, w_ref) in enumerate(((fa_ref, wg0_ref, wa_ref), (fb_ref, wg1_ref, wb_ref),
                                                   (fc_ref, wg2_ref, wc_ref))):
            gate = _sigmoid(_dot(xb, wg_ref[...]) + bg_ref[:, j * D_MODEL:(j + 1) * D_MODEL])
            term = gate * _dot(f_ref[_sub(s), :], w_ref[...])
            m = term if m is None else m + term
        z = _dot(m.astype(BF16), wo_ref[...])
        o_ref[_sub(s), :] = _layer_norm(DEEPNORM_ALPHA * x + z, g_ref[...], b_ref[...])


def _merge_call(x, fa, fb, fc, w_in_b, b_gate, w_a, w_b, w_c, w_o, ln_g, ln_b, layer, tm=ROW_TILE):
    t = x.shape[0]
    row = lambda i: (i, 0)
    vec = lambda n: pl.BlockSpec((1, n), lambda i: (0, 0))
    feat = pl.BlockSpec((tm, BRANCH_W), row)
    gate_cols = [_layer_weight((D_MODEL, D_MODEL), layer, PROJ_COLS // D_MODEL + j) for j in range(N_BRANCHES)]
    branch_w = _layer_weight((BRANCH_W, D_MODEL), layer)
    return pl.pallas_call(
        _merge_kernel,
        grid=(t // tm,),
        in_specs=[pl.BlockSpec((tm, D_MODEL), row), feat, feat, feat, *gate_cols, vec(N_BRANCHES * D_MODEL),
                  branch_w, branch_w, branch_w, _layer_weight((D_MODEL, D_MODEL), layer),
                  vec(D_MODEL), vec(D_MODEL)],
        out_specs=pl.BlockSpec((tm, D_MODEL), row),
        out_shape=jax.ShapeDtypeStruct((t, D_MODEL), F32),
        compiler_params=_params(1),
        name="merge",
    )(x, fa, fb, fc, w_in_b, w_in_b, w_in_b, b_gate, w_a, w_b, w_c, w_o, ln_g, ln_b)


FF_CHUNK = 1024


def _ffn_kernel(x_ref, w1_ref, b1_ref, w2_ref, b2_ref, g_ref, b_ref, o_ref):
    for s in range(x_ref.shape[0] // SUB_TILE):
        x = x_ref[_sub(s), :]
        xb = x.astype(BF16)
        y = None
        for c in range(D_FF // FF_CHUNK):
            cols = slice(c * FF_CHUNK, (c + 1) * FF_CHUNK)
            h = jnp.maximum(_dot(xb, w1_ref[:, cols]) + b1_ref[:, cols], 0.0)
            part = _dot((h * h).astype(BF16), w2_ref[cols, :])
            y = part if y is None else y + part
        o_ref[_sub(s), :] = _layer_norm(DEEPNORM_ALPHA * x + y + b2_ref[...], g_ref[...], b_ref[...])


def _ffn_call(x, w1, b1, w2, b2, ln_g, ln_b, layer, tm=ROW_TILE):
    t = x.shape[0]
    row = lambda i: (i, 0)
    vec = lambda n: pl.BlockSpec((1, n), lambda i: (0, 0))
    return pl.pallas_call(
        _ffn_kernel,
        grid=(t // tm,),
        in_specs=[pl.BlockSpec((tm, D_MODEL), row),
                  _layer_weight((D_MODEL, D_FF), layer), vec(D_FF), _layer_weight((D_FF, D_MODEL), layer),
                  vec(D_MODEL), vec(D_MODEL), vec(D_MODEL)],
        out_specs=pl.BlockSpec((tm, D_MODEL), row),
        out_shape=jax.ShapeDtypeStruct((t, D_MODEL), F32),
        compiler_params=_params(1),
        name="ffn",
    )(x, w1, b1, w2, b2, ln_g, ln_b)


def _rope_tables(seq):
    pos = jnp.arange(seq, dtype=F32)
    inv = ROPE_THETA ** (-jnp.arange(0, DIFF_QK_DIM, 2, dtype=F32) / DIFF_QK_DIM)
    ang = pos[:, None] * inv[None, :]
    ang = jnp.concatenate([ang, ang], axis=-1)
    cos, sin = jnp.cos(ang), jnp.sin(ang)
    first_half = (jnp.arange(DIFF_QK_DIM) < HALF)[None, :]
    sin_a = jnp.where(first_half, -sin, 0.0)
    sin_b = jnp.where(first_half, 0.0, sin)
    tile2 = lambda a: jnp.concatenate([a, a], axis=-1)
    k_tabs = [tile2(cos), tile2(sin_a), tile2(sin_b)]
    scale = DIFF_QK_DIM ** -0.5 * math.log2(math.e)
    q_tabs = [a * scale for a in k_tabs]
    return q_tabs + k_tabs


def kernel(x, w_in, b_gate, conv_w, conv_b, conv_ln_g, conv_ln_b, w_conv_out, lam_q1, lam_k1, lam_q2, lam_k2,
           diff_subln_g, w_diff_out, w_moba_out, w_o, ln1_g, ln1_b, w_ff1, b_ff1, w_ff2, b_ff2, ln2_g, ln2_b):
    batch, seq, d = x.shape
    assert d == D_MODEL and seq % ROW_TILE == 0 and seq % CONV_TILE == 0 and seq % ATT_BLOCK == 0
    tables = _rope_tables(seq)
    row = lambda a: a.reshape(1, -1)
    xs = x.reshape(batch * seq, d)
    w_in_b, w_a, w_b, w_c, w_o_b, w1, w2 = (w.astype(BF16) for w in
                                            (w_in, w_conv_out, w_diff_out, w_moba_out, w_o, w_ff1, w_ff2))
    for l in range(DEPTH):
        h, dq, dk, dv, mq, mk, mv = _proj_call(xs, w_in_b, tables, seq, l)
        fa = _conv_call(h, conv_w[l, :, 0, :], row(conv_b[l]), row(conv_ln_g[l]), row(conv_ln_b[l]), batch, seq)
        fb = _diff_call(dq, dk, dv, row(lam_q1[l]), row(lam_k1[l]), row(lam_q2[l]), row(lam_k2[l]),
                        row(diff_subln_g[l]), l, batch, seq)
        fc = _moba_call(mq, mk, mv, batch, seq)
        xs = _merge_call(xs, fa, fb, fc, w_in_b, row(b_gate[l]), w_a, w_b, w_c, w_o_b,
                         row(ln1_g[l]), row(ln1_b[l]), l)
        xs = _ffn_call(xs, w1, row(b_ff1[l]), w2, row(b_ff2[l]), row(ln2_g[l]), row(ln2_b[l]), l)
    return xs.reshape(batch, seq, d)
```

```python
import functools
import math

import jax
import jax.numpy as jnp
from jax import lax
from jax.experimental import pallas as pl
from jax.experimental.pallas import tpu as pltpu

F32 = jnp.float32
BF16 = jnp.bfloat16

D_MODEL = 1024
DEPTH = 4
CONV_CH = D_MODEL // 2
CONV_WIDTH = 31
DIFF_QK_DIM = 64
DIFF_V_DIM = 2 * DIFF_QK_DIM
DIFF_HEADS = D_MODEL // 256
MOBA_HEAD_DIM = 64
MOBA_HEADS = D_MODEL // 128
MOBA_BLOCK = 256
MOBA_TOPK = 3
D_FF = 4 * D_MODEL
ROPE_THETA = 10000.0
N_BRANCHES = 3
LN_EPS = 1e-5
NEG = -1e30
DEEPNORM_ALPHA = (2.0 * DEPTH) ** 0.25

BRANCH_W = 512
PROJ_COLS = 8 * BRANCH_W
LANES = 128
SUBLANES = 8
HALF = DIFF_QK_DIM // 2
ATT_BLOCK = 256
SCORE_AHEAD = 3
SCORE_BUFFERS = SCORE_AHEAD + 1
ROW_TILE = 1024
SUB_TILE = 256
PROJ_SUB_TILE = 512
CONV_SUB_TILE = 256
CONV_HALO = 32
VMEM_LIMIT = 56 * 1024 * 1024


def _params(n_axes):
    return pltpu.CompilerParams(dimension_semantics=("arbitrary",) * n_axes,
                                vmem_limit_bytes=VMEM_LIMIT)


def _layer_weight(shape, layer, col_block=0):
    return pl.BlockSpec((None,) + shape, lambda i: (layer, 0, col_block), pipeline_mode=pl.Buffered(1))


def _sub(s):
    return slice(s * SUB_TILE, (s + 1) * SUB_TILE)


def _sigmoid(x):
    return 1.0 / (1.0 + jnp.exp(-x))


def _layer_norm(y, g, b):
    mu = jnp.mean(y, axis=-1, keepdims=True)
    d = y - mu
    var = jnp.mean(d * d, axis=-1, keepdims=True)
    return d * lax.rsqrt(var + LN_EPS) * g + b


def _dot(a, b):
    return jnp.dot(a, b, preferred_element_type=F32)


def _dot_nt(a, b):
    return lax.dot_general(a, b, (((1,), (1,)), ((), ())), preferred_element_type=F32)


def _rope(u, c, sa, sb):
    outs = []
    for j in range(u.shape[1] // LANES):
        uc = u[:, LANES * j:LANES * (j + 1)]
        fwd = pltpu.roll(uc, HALF, 1)
        bwd = pltpu.roll(uc, LANES - HALF, 1)
        outs.append(uc * c + bwd * sa + fwd * sb)
    return jnp.concatenate(outs, axis=1)


def _proj_kernel(x_ref, w_ref, cq_ref, sqa_ref, sqb_ref, ck_ref, ska_ref, skb_ref,
                 cw_ref, cb_ref, cg_ref, cbeta_ref,
                 fa_ref, dq_ref, dk_ref, dv_ref, mq_ref, mk_ref, mv_ref, pad_ref, xb_ref, *, tiles_per_seq):
    n_rows = x_ref.shape[0]
    starts_sequence = pl.program_id(0) % tiles_per_seq == 0

    @pl.when(starts_sequence)
    def _():
        pad_ref[0:CONV_HALO, :] = jnp.zeros((CONV_HALO, CONV_CH), F32)

    @pl.when(jnp.logical_not(starts_sequence))
    def _():
        pad_ref[0:CONV_HALO, :] = pad_ref[n_rows:n_rows + CONV_HALO, :]

    xb_ref[...] = x_ref[...].astype(BF16)
    never = pl.program_id(0) < 0
    pace = [None]

    def sub_tile(s):
        rows = slice(s * PROJ_SUB_TILE, (s + 1) * PROJ_SUB_TILE)

        def mm(group):
            u = _dot(xb_ref[rows, :], w_ref[:, group * BRANCH_W:(group + 1) * BRANCH_W])
            pace[0] = jnp.where(never, u[-1:, :], 0.0)
            return u
        return rows, mm

    def glu(s):
        rows, mm = sub_tile(s)
        pad_ref[CONV_HALO + rows.start:CONV_HALO + rows.stop, :] = mm(0) * _sigmoid(mm(1))

    def conv_pieces():
        base = CONV_HALO - (CONV_WIDTH - 1)
        for c in range(n_rows // CONV_SUB_TILE):
            first = c * CONV_SUB_TILE
            acc = jnp.zeros((CONV_SUB_TILE, CONV_CH), F32) + cb_ref[...]
            for r in range(SUBLANES):
                taps = [k for k in range(CONV_WIDTH) if (base + k) % SUBLANES == r]
                last = base + taps[-1]
                slab = pad_ref[first + r:first + last + CONV_SUB_TILE, :]
                part = None
                for k in taps:
                    off = base + k - r
                    w_row = cw_ref[k:k + 1, :]
                    if part is None:
                        w_row = w_row + pace[0]
                    term = slab[off:off + CONV_SUB_TILE, :] * w_row
                    part = term if part is None else part + term
                acc = acc + part
                yield
            y = _layer_norm(acc, cg_ref[...], cbeta_ref[...])
            fa_ref[first:first + CONV_SUB_TILE, :] = (y * _sigmoid(y)).astype(BF16)
            yield

    def rope_group(s, group, out_ref, tabs):
        rows, mm = sub_tile(s)
        out_ref[rows, :] = _rope(mm(group), *(t[rows, :] for t in tabs)).astype(BF16)

    def plain_group(s, group, out_ref):
        rows, mm = sub_tile(s)
        out_ref[rows, :] = mm(group).astype(BF16)

    n_sub = n_rows // PROJ_SUB_TILE
    q_tabs, k_tabs = (cq_ref, sqa_ref, sqb_ref), (ck_ref, ska_ref, skb_ref)
    glu(0)
    matmuls = [functools.partial(glu, s) for s in range(1, n_sub)]
    for s in range(n_sub):
        matmuls += [functools.partial(rope_group, s, 2, dq_ref, q_tabs),
                    functools.partial(rope_group, s, 3, dk_ref, k_tabs),
                    functools.partial(plain_group, s, 4, dv_ref),
                    functools.partial(rope_group, s, 5, mq_ref, q_tabs),
                    functools.partial(rope_group, s, 6, mk_ref, k_tabs),
                    functools.partial(plain_group, s, 7, mv_ref)]
    pieces = conv_pieces()
    n_pieces = (n_rows // CONV_SUB_TILE) * (SUBLANES + 1)
    per_matmul = -(-n_pieces // len(matmuls))
    for matmul in matmuls:
        matmul()
        for _ in range(per_matmul):
            next(pieces, None)
    for _ in pieces:
        pass


def _proj_call(x, w_in_b, tables, conv_w, conv_b, conv_ln_g, conv_ln_b, seq, layer, tm=ROW_TILE):
    t = x.shape[0]
    tiles_per_seq = seq // tm
    row = lambda i: (i, 0)
    tab = lambda i: (i % tiles_per_seq, 0)
    vec = pl.BlockSpec((1, CONV_CH), lambda i: (0, 0))
    out_spec = pl.BlockSpec((tm, BRANCH_W), row)
    return pl.pallas_call(
        functools.partial(_proj_kernel, tiles_per_seq=tiles_per_seq),
        grid=(t // tm,),
        in_specs=[pl.BlockSpec((tm, D_MODEL), row), _layer_weight((D_MODEL, PROJ_COLS), layer)]
                 + [pl.BlockSpec((tm, LANES), tab)] * 6
                 + [pl.BlockSpec((CONV_WIDTH, CONV_CH), lambda i: (0, 0)), vec, vec, vec],
        out_specs=[out_spec] * 7,
        out_shape=[jax.ShapeDtypeStruct((t, BRANCH_W), BF16)] * 7,
        scratch_shapes=[pltpu.VMEM((CONV_HALO + tm, CONV_CH), F32), pltpu.VMEM((tm, D_MODEL), BF16)],
        compiler_params=_params(1),
        name="proj",
    )(x, w_in_b, *tables, conv_w, conv_b, conv_ln_g, conv_ln_b)


MAPS = 2
ROWS = MAPS * ATT_BLOCK


def _blk(kb):
    return slice(kb * ATT_BLOCK, (kb + 1) * ATT_BLOCK)


def _causal_mask():
    r = lax.broadcasted_iota(jnp.int32, (ROWS, ATT_BLOCK), 0) % ATT_BLOCK
    c = lax.broadcasted_iota(jnp.int32, (ROWS, ATT_BLOCK), 1)
    return c <= r


def _fill_v_aug(v_ref, v_aug_ref):
    v_aug_ref[:, :LANES] = v_ref[...]
    v_aug_ref[:, LANES:] = jnp.ones((v_ref.shape[0], LANES), BF16)


def _tile_scores(tile, score_fn, s_ref):
    m_part = None
    for kb in range(tile + 1):
        s = score_fn(kb)
        if kb == tile:
            s = jnp.where(_causal_mask(), s, NEG)
        s_ref[:, _blk(kb)] = s
        folded = jnp.maximum(s[:, :LANES], s[:, LANES:])
        m_part = folded if m_part is None else jnp.maximum(m_part, folded)
    return jnp.max(m_part, axis=-1, keepdims=True)


def _pipelined_tiles(order, scores, s_ref, p_ref, v_aug_ref, finish, after_prologue=None):
    buf = {tile: pos % SCORE_BUFFERS for pos, tile in enumerate(order)}
    maxima = {tile: scores(tile, s_ref.at[buf[tile]]) for tile in order[:SCORE_AHEAD]}
    if after_prologue is not None:
        after_prologue()
    for pos, tile in enumerate(order):
        m, b = maxima.pop(tile), buf[tile]
        for kb in range(tile + 1):
            p_ref[b, :, _blk(kb)] = jnp.exp2(s_ref[b, :, _blk(kb)] - m).astype(BF16)
        if pos + SCORE_AHEAD < len(order):
            ahead = order[pos + SCORE_AHEAD]
            maxima[ahead] = scores(ahead, s_ref.at[buf[ahead]])
        n_keys = (tile + 1) * ATT_BLOCK
        acc = _dot(p_ref[b, :, :n_keys], v_aug_ref[:n_keys, :])
        finish(tile, acc[:, :LANES] / acc[:, LANES:])


def _att_scratch(seq):
    return [pltpu.VMEM((SCORE_BUFFERS, ROWS, seq), F32),
            pltpu.VMEM((SCORE_BUFFERS, ROWS, seq), BF16),
            pltpu.VMEM((seq, 2 * LANES), BF16)]


def _diff_kernel(q_ref, k_ref, v_ref, lq1_ref, lk1_ref, lq2_ref, lk2_ref, g_ref, o_ref,
                 s_ref, p_ref, v_aug_ref, *, lambda_init):
    lam = (jnp.exp(jnp.sum(lq1_ref[...] * lk1_ref[...], axis=-1, keepdims=True))
           - jnp.exp(jnp.sum(lq2_ref[...] * lk2_ref[...], axis=-1, keepdims=True)) + lambda_init)
    _fill_v_aug(v_ref, v_aug_ref)
    lane = lax.broadcasted_iota(jnp.int32, (ATT_BLOCK, LANES), 1)

    def scores(tile, s_buf):
        qf = q_ref[_blk(tile), :].astype(F32)
        qq = jnp.concatenate([jnp.where(lane < DIFF_QK_DIM, qf, 0.0),
                              jnp.where(lane >= DIFF_QK_DIM, qf, 0.0)], axis=0).astype(BF16)
        return _tile_scores(tile, lambda kb: _dot_nt(qq, k_ref[_blk(kb), :]), s_buf)

    def finish(tile, out):
        o = out[:ATT_BLOCK] - lam * out[ATT_BLOCK:]
        o = o * lax.rsqrt(jnp.mean(o * o, axis=-1, keepdims=True) + LN_EPS) * g_ref[...]
        o_ref[_blk(tile), :] = (o * (1.0 - lambda_init)).astype(BF16)

    _pipelined_tiles(list(reversed(range(k_ref.shape[0] // ATT_BLOCK))), scores, s_ref, p_ref, v_aug_ref, finish)


def _diff_call(dq, dk, dv, lq1, lk1, lq2, lk2, subln_g, layer_idx, batch, seq):
    t = dq.shape[0]
    lambda_init = 0.8 - 0.6 * math.exp(-0.3 * layer_idx)
    lam_spec = pl.BlockSpec((1, DIFF_QK_DIM), lambda b, h: (0, 0))
    seq_block = pl.BlockSpec((seq, LANES), lambda b, h: (b, h))
    return pl.pallas_call(
        functools.partial(_diff_kernel, lambda_init=lambda_init),
        grid=(batch, DIFF_HEADS),
        in_specs=[seq_block, seq_block, seq_block,
                  lam_spec, lam_spec, lam_spec, lam_spec,
                  pl.BlockSpec((1, DIFF_V_DIM), lambda b, h: (0, 0))],
        out_specs=seq_block,
        out_shape=jax.ShapeDtypeStruct((t, BRANCH_W), BF16),
        scratch_shapes=_att_scratch(seq),
        compiler_params=_params(2),
        name="diff_attn",
    )(dq, dk, dv, lq1, lk1, lq2, lk2, subln_g)


def _moba_select(gate, first, own):
    lane = lax.broadcasted_iota(jnp.int32, gate.shape, 1)
    lane_f = lane.astype(F32)
    neg_inf = -jnp.inf
    cand = jnp.logical_and(lane >= first, lane < first + own)
    gv = jnp.where(cand, gate, neg_inf)
    for _ in range(MOBA_TOPK):
        mx = jnp.max(gv, axis=-1, keepdims=True)
        is_mx = jnp.logical_and(gv == mx, gv > neg_inf)
        idx = jnp.min(jnp.where(is_mx, lane_f, float(LANES)), axis=-1, keepdims=True)
        gv = jnp.where(lane_f == idx, neg_inf, gv)
    return jnp.where(gv > neg_inf, NEG, 0.0)


def _moba_kernel(q_ref, k_ref, v_ref, o_ref, s_ref, p_ref, v_aug_ref, k_aug_ref, q_aug_ref, kmean_ref):
    seq = k_ref.shape[0]
    n_blocks = seq // MOBA_BLOCK
    _fill_v_aug(v_ref, v_aug_ref)

    kf = k_ref[...].astype(F32)
    k_lane = lax.broadcasted_iota(jnp.int32, (seq, LANES), 1)
    k_block = lax.broadcasted_iota(jnp.int32, (seq, LANES), 0) // MOBA_BLOCK
    low = k_lane < MOBA_HEAD_DIM
    k_aug_ref[0] = jnp.where(low, kf, (k_lane - MOBA_HEAD_DIM == k_block).astype(F32)).astype(BF16)
    k_aug_ref[1] = jnp.where(low, (k_lane == k_block).astype(F32), kf).astype(BF16)

    kmean_ref[...] = jnp.zeros(kmean_ref.shape, F32)
    for n in range(n_blocks - 1):
        mean = jnp.sum(kf[_blk(n)], axis=0, keepdims=True) * (1.0 / MOBA_BLOCK)
        kmean_ref[n:n + 1, :] = mean
        kmean_ref[MOBA_HEAD_DIM + n:MOBA_HEAD_DIM + n + 1, :] = mean
    kmean = kmean_ref[...]
    kmean_hi = kmean.astype(BF16)
    kmean_lo = (kmean - kmean_hi.astype(F32)).astype(BF16)

    lane = lax.broadcasted_iota(jnp.int32, (ATT_BLOCK, LANES), 1)

    def fill_queries(own):
        qf = q_ref[_blk(own), :].astype(F32)
        q_even = jnp.where(lane < MOBA_HEAD_DIM, qf, 0.0)
        q_odd = jnp.where(lane >= MOBA_HEAD_DIM, qf, 0.0)
        if own > MOBA_TOPK:
            def gate(qh):
                qb = qh.astype(BF16)
                return _dot_nt(qb, kmean_hi) + _dot_nt(qb, kmean_lo)
            q_even = q_even + _moba_select(gate(q_even), MOBA_HEAD_DIM, own)
            q_odd = q_odd + _moba_select(gate(q_odd), 0, own)
        q_aug_ref[0, _blk(own), :] = q_even.astype(BF16)
        q_aug_ref[1, _blk(own), :] = q_odd.astype(BF16)

    def scores(own, s_buf):
        def block(kb):
            return jnp.concatenate([_dot_nt(q_aug_ref[0, _blk(own), :], k_aug_ref[0, _blk(kb), :]),
                                    _dot_nt(q_aug_ref[1, _blk(own), :], k_aug_ref[1, _blk(kb), :])], axis=0)
        return _tile_scores(own, block, s_buf)

    def finish(own, out):
        o_ref[_blk(own), :] = jnp.where(lane < MOBA_HEAD_DIM, out[:ATT_BLOCK], out[ATT_BLOCK:]).astype(BF16)

    plain = [own for own in range(n_blocks) if own <= MOBA_TOPK]
    selecting = [own for own in range(n_blocks) if own > MOBA_TOPK]
    for own in plain:
        fill_queries(own)
    first = plain[::-1][:SCORE_AHEAD]
    order = first + selecting[::-1] + [own for own in plain[::-1] if own not in first]
    _pipelined_tiles(order, scores, s_ref, p_ref, v_aug_ref, finish,
                     after_prologue=lambda: [fill_queries(own) for own in selecting])


def _moba_call(mq, mk, mv, batch, seq):
    t = mq.shape[0]
    seq_block = pl.BlockSpec((seq, LANES), lambda b, h: (b, h))
    return pl.pallas_call(
        _moba_kernel,
        grid=(batch, MOBA_HEADS // 2),
        in_specs=[seq_block, seq_block, seq_block],
        out_specs=seq_block,
        out_shape=jax.ShapeDtypeStruct((t, BRANCH_W), BF16),
        scratch_shapes=_att_scratch(seq) + [pltpu.VMEM((2, seq, LANES), BF16),
                                            pltpu.VMEM((2, seq, LANES), BF16),
                                            pltpu.VMEM((LANES, LANES), F32)],
        compiler_params=_params(2),
        name="moba",
    )(mq, mk, mv)


def _merge_kernel(x_ref, fa_ref, fb_ref, fc_ref, wg0_ref, wg1_ref, wg2_ref, bg_ref, wa_ref, wb_ref, wc_ref,
                  wo_ref, g_ref, b_ref, o_ref):
    for s in range(x_ref.shape[0] // SUB_TILE):
        x = x_ref[_sub(s), :]
        xb = x.astype(BF16)
        m = None
        for j, (f_ref, wg_ref, w_ref) in enumerate(((fa_ref, wg0_ref, wa_ref), (fb_ref, wg1_ref, wb_ref),
                                                   (fc_ref, wg2_ref, wc_ref))):
            gate = _sigmoid(_dot(xb, wg_ref[...]) + bg_ref[:, j * D_MODEL:(j + 1) * D_MODEL])
            term = gate * _dot(f_ref[_sub(s), :], w_ref[...])
            m = term if m is None else m + term
        z = _dot(m.astype(BF16), wo_ref[...])
        o_ref[_sub(s), :] = _layer_norm(DEEPNORM_ALPHA * x + z, g_ref[...], b_ref[...])


def _merge_call(x, fa, fb, fc, w_in_b, b_gate, w_a, w_b, w_c, w_o, ln_g, ln_b, layer, tm=ROW_TILE):
    t = x.shape[0]
    row = lambda i: (i, 0)
    vec = lambda n: pl.BlockSpec((1, n), lambda i: (0, 0))
    feat = pl.BlockSpec((tm, BRANCH_W), row)
    gate_cols = [_layer_weight((D_MODEL, D_MODEL), layer, PROJ_COLS // D_MODEL + j) for j in range(N_BRANCHES)]
    branch_w = _layer_weight((BRANCH_W, D_MODEL), layer)
    return pl.pallas_call(
        _merge_kernel,
        grid=(t // tm,),
        in_specs=[pl.BlockSpec((tm, D_MODEL), row), feat, feat, feat, *gate_cols, vec(N_BRANCHES * D_MODEL),
                  branch_w, branch_w, branch_w, _layer_weight((D_MODEL, D_MODEL), layer),
                  vec(D_MODEL), vec(D_MODEL)],
        out_specs=pl.BlockSpec((tm, D_MODEL), row),
        out_shape=jax.ShapeDtypeStruct((t, D_MODEL), F32),
        compiler_params=_params(1),
        name="merge",
    )(x, fa, fb, fc, w_in_b, w_in_b, w_in_b, b_gate, w_a, w_b, w_c, w_o, ln_g, ln_b)


FF_CHUNK = 1024


def _ffn_kernel(x_ref, w1_ref, b1_ref, w2_ref, b2_ref, g_ref, b_ref, o_ref):
    for s in range(x_ref.shape[0] // SUB_TILE):
        x = x_ref[_sub(s), :]
        xb = x.astype(BF16)
        y = None
        for c in range(D_FF // FF_CHUNK):
            cols = slice(c * FF_CHUNK, (c + 1) * FF_CHUNK)
            h = jnp.maximum(_dot(xb, w1_ref[:, cols]) + b1_ref[:, cols], 0.0)
            part = _dot((h * h).astype(BF16), w2_ref[cols, :])
            y = part if y is None else y + part
        o_ref[_sub(s), :] = _layer_norm(DEEPNORM_ALPHA * x + y + b2_ref[...], g_ref[...], b_ref[...])


def _ffn_call(x, w1, b1, w2, b2, ln_g, ln_b, layer, tm=ROW_TILE):
    t = x.shape[0]
    row = lambda i: (i, 0)
    vec = lambda n: pl.BlockSpec((1, n), lambda i: (0, 0))
    return pl.pallas_call(
        _ffn_kernel,
        grid=(t // tm,),
        in_specs=[pl.BlockSpec((tm, D_MODEL), row),
                  _layer_weight((D_MODEL, D_FF), layer), vec(D_FF), _layer_weight((D_FF, D_MODEL), layer),
                  vec(D_MODEL), vec(D_MODEL), vec(D_MODEL)],
        out_specs=pl.BlockSpec((tm, D_MODEL), row),
        out_shape=jax.ShapeDtypeStruct((t, D_MODEL), F32),
        compiler_params=_params(1),
        name="ffn",
    )(x, w1, b1, w2, b2, ln_g, ln_b)


def _rope_tables(seq):
    pos = jnp.arange(seq, dtype=F32)
    inv = ROPE_THETA ** (-jnp.arange(0, DIFF_QK_DIM, 2, dtype=F32) / DIFF_QK_DIM)
    ang = pos[:, None] * inv[None, :]
    ang = jnp.concatenate([ang, ang], axis=-1)
    cos, sin = jnp.cos(ang), jnp.sin(ang)
    first_half = (jnp.arange(DIFF_QK_DIM) < HALF)[None, :]
    sin_a = jnp.where(first_half, -sin, 0.0)
    sin_b = jnp.where(first_half, 0.0, sin)
    tile2 = lambda a: jnp.concatenate([a, a], axis=-1)
    k_tabs = [tile2(cos), tile2(sin_a), tile2(sin_b)]
    scale = DIFF_QK_DIM ** -0.5 * math.log2(math.e)
    q_tabs = [a * scale for a in k_tabs]
    return q_tabs + k_tabs


def kernel(x, w_in, b_gate, conv_w, conv_b, conv_ln_g, conv_ln_b, w_conv_out, lam_q1, lam_k1, lam_q2, lam_k2,
           diff_subln_g, w_diff_out, w_moba_out, w_o, ln1_g, ln1_b, w_ff1, b_ff1, w_ff2, b_ff2, ln2_g, ln2_b):
    batch, seq, d = x.shape
    assert d == D_MODEL and seq % ROW_TILE == 0 and seq % ATT_BLOCK == 0
    tables = _rope_tables(seq)
    row = lambda a: a.reshape(1, -1)
    xs = x.reshape(batch * seq, d)
    w_in_b, w_a, w_b, w_c, w_o_b, w1, w2 = (w.astype(BF16) for w in
                                            (w_in, w_conv_out, w_diff_out, w_moba_out, w_o, w_ff1, w_ff2))
    for l in range(DEPTH):
        fa, dq, dk, dv, mq, mk, mv = _proj_call(xs, w_in_b, tables, conv_w[l, :, 0, :], row(conv_b[l]),
                                                row(conv_ln_g[l]), row(conv_ln_b[l]), seq, l)
        fb = _diff_call(dq, dk, dv, row(lam_q1[l]), row(lam_k1[l]), row(lam_q2[l]), row(lam_k2[l]),
                        row(diff_subln_g[l]), l, batch, seq)
        fc = _moba_call(mq, mk, mv, batch, seq)
        xs = _merge_call(xs, fa, fb, fc, w_in_b, row(b_gate[l]), w_a, w_b, w_c, w_o_b,
                         row(ln1_g[l]), row(ln1_b[l]), l)
        xs = _ffn_call(xs, w1, row(b_ff1[l]), w2, row(b_ff2[l]), row(ln2_g[l]), row(ln2_b[l]), l)
    return xs.reshape(batch, seq, d)
```
